```python
import math
import jax, jax.numpy as jnp
from jax import lax
import numpy as np

D_MODEL = 1024
BATCH = 8
SEQ = 4096
DEPTH = 1

GRID_W = 64
CTX_LEN = 256
POOL_GROUPS = 4
POOL_WINDOWS = (2, 4, 8, 16)
POOL_WIDTH = D_MODEL
POOL_GROUP_DIM = POOL_WIDTH // POOL_GROUPS
M_HEADS = 4
M_HEAD_DIM = D_MODEL // M_HEADS
M_WIDTH = M_HEADS * M_HEAD_DIM
CONV_W = 3
CHUNK = 128
D_FF = -(-8 * D_MODEL // (3 * 256)) * 256
N_GATE_COLS = 4 * M_HEADS
IN_COLS = POOL_WIDTH + 4 * M_WIDTH + 2 * D_MODEL + N_GATE_COLS
EPS = 1e-6

kernel_name = "hybrid_pool_mlstm_dit_prefix"


def _rmsnorm(x, gain):
    xf = x.astype(jnp.float32)
    y = xf * lax.rsqrt(jnp.mean(xf * xf, axis=-1, keepdims=True) + EPS)
    return y.astype(x.dtype) * gain


def _modulate(x, gain, shift, scale):
    return _rmsnorm(x, gain) * (1 + scale) + shift


def _swiglu(h, w_in, w_out):
    g, u = jnp.split(h @ w_in, 2, axis=-1)
    return (jax.nn.silu(g) * u) @ w_out


def _window_sum(x, w, axis):
    n = x.shape[axis]
    cs = jnp.cumsum(x, axis=axis)
    zero = jnp.zeros_like(lax.slice_in_dim(cs, 0, 1, axis=axis))
    cs = jnp.concatenate([zero, cs], axis=axis)
    t = jnp.arange(n)
    lo = jnp.clip(t - w // 2, 0, n)
    hi = jnp.clip(t + w // 2, 0, n)
    s = jnp.take(cs, hi, axis=axis) - jnp.take(cs, lo, axis=axis)
    return s, (hi - lo).astype(x.dtype)


def _pool_mixer(u, w_pool, scale, on_grid):
    B, N, _ = u.shape
    G = POOL_GROUP_DIM
    uf = u.astype(jnp.float32)
    outs = []
    for g, w in enumerate(POOL_WINDOWS):
        ug = uf[..., g * G:(g + 1) * G]
        if on_grid:
            rows = N // GRID_W
            ug2 = ug.reshape(B, rows, GRID_W, G)
            s_r, cnt_r = _window_sum(ug2, w, 1)
            s_rc, cnt_c = _window_sum(s_r, w, 2)
            mean = (s_rc / (cnt_r[:, None] * cnt_c[None, :])[None, :, :, None]).reshape(B, N, G)
        else:
            s, cnt = _window_sum(ug, w, 1)
            mean = s / cnt[None, :, None]
        outs.append(jnp.einsum('bnc,cd->bnd', (mean - ug).astype(u.dtype), w_pool[g]))
    return jnp.concatenate(outs, axis=-1) * scale


def _short_conv(x, w, b):
    n = x.shape[1]
    pad = CONV_W // 2
    xp = jnp.pad(x, ((0, 0), (pad, pad), (0, 0)))
    y = b
    for j in range(CONV_W):
        y = y + xp[:, j:j + n] * w[j]
    return y


def _to_heads(a):
    B, N, _ = a.shape
    return a.reshape(B, N, M_HEADS, M_HEAD_DIM).transpose(0, 2, 1, 3).astype(jnp.float32)


def _flip_t(a):
    return jnp.flip(a, axis=2)


def _mlstm_prepare(q, k, v, gates_pre, conv_w, conv_b, b_gates):
    qk = jax.nn.silu(_short_conv(jnp.concatenate([q, k], axis=-1), conv_w, conv_b))
    q, k = jnp.split(qk, 2, axis=-1)
    g = (gates_pre + b_gates).astype(jnp.float32).transpose(0, 2, 1)
    i_f, f_f, i_b, f_b = jnp.split(g, 4, axis=1)
    return _to_heads(q) * M_HEAD_DIM ** -0.5, _to_heads(k), _to_heads(v), (i_f, f_f, i_b, f_b)


def _mlstm_scan(q, k, v, i_pre, f_pre, state):
    B, H, N, dk = q.shape
    dv = v.shape[-1]
    L = CHUNK
    nc = N // L
    logf = jax.nn.log_sigmoid(f_pre)

    def to_chunks(a):
        return jnp.moveaxis(a.reshape(B, H, nc, L, *a.shape[3:]), 2, 0)

    mask = jnp.tril(jnp.ones((L, L), dtype=bool))

    def step(carry, inp):
        C0, n0, m0 = carry
        qc, kc, vc, ic, fc = inp
        b = jnp.cumsum(fc, axis=-1)
        a = b + m0[..., None]
        Dm = jnp.where(mask, b[..., :, None] - b[..., None, :] + ic[..., None, :], -jnp.inf)
        m = jnp.maximum(a, jnp.max(Dm, axis=-1))
        S = jnp.einsum('bhtd,bhsd->bhts', qc, kc) * jnp.exp(Dm - m[..., None])
        sa = jnp.exp(a - m)
        num = jnp.einsum('bhts,bhsv->bhtv', S, vc) + sa[..., None] * jnp.einsum('bhtd,bhdv->bhtv', qc, C0)
        den = jnp.sum(S, axis=-1) + sa * jnp.einsum('bhtd,bhd->bht', qc, n0)
        h = num / jnp.maximum(jnp.abs(den), jnp.exp(-m))[..., None]
        bL = b[..., -1]
        w = bL[..., None] - b + ic
        m_new = jnp.maximum(bL + m0, jnp.max(w, axis=-1))
        ws = jnp.exp(w - m_new[..., None])
        s0 = jnp.exp(bL + m0 - m_new)
        C_new = s0[..., None, None] * C0 + jnp.einsum('bhsd,bhsv->bhdv', kc * ws[..., None], vc)
        n_new = s0[..., None] * n0 + jnp.einsum('bhs,bhsd->bhd', ws, kc)
        return (C_new, n_new, m_new), h

    state, hs = lax.scan(step, state, (to_chunks(q), to_chunks(k), to_chunks(v), to_chunks(i_pre), to_chunks(logf)))
    h = jnp.moveaxis(hs, 0, 2).reshape(B, H, N, dv)
    return h, state


def _mlstm_bidir(q, k, v, gates, state_f, state_b):
    i_f, f_f, i_b, f_b = gates
    h_f, st_f = _mlstm_scan(q, k, v, i_f, f_f, state_f)
    h_b, st_b = _mlstm_scan(_flip_t(q), _flip_t(k), _flip_t(v), _flip_t(i_b), _flip_t(f_b), state_b)
    return h_f + _flip_t(h_b), st_f, st_b


def _mlstm_readout(h, o, gain):
    B, H, N, dh = h.shape
    h = h.transpose(0, 2, 1, 3)
    h = h * lax.rsqrt(jnp.mean(h * h, axis=-1, keepdims=True) + EPS)
    h = h.reshape(B, N, H * dh).astype(o.dtype) * gain
    return h * jax.nn.sigmoid(o)


def _token_mixer(a_lat, a_ctx, w_in, b_gates, conv_w, conv_b, w_pool, pool_scale, mh_gain, w_out, with_ctx_out):
    split_at = np.cumsum([POOL_WIDTH, M_WIDTH, M_WIDTH, M_WIDTH, M_WIDTH, D_MODEL, D_MODEL]).tolist()
    p_lat = jnp.split(a_lat @ w_in, split_at, axis=-1)
    p_ctx = jnp.split(a_ctx @ w_in, split_at, axis=-1)
    qc, kc, vc, gc = _mlstm_prepare(p_ctx[1], p_ctx[2], p_ctx[3], p_ctx[7], conv_w, conv_b, b_gates)
    ql, kl, vl, gl = _mlstm_prepare(p_lat[1], p_lat[2], p_lat[3], p_lat[7], conv_w, conv_b, b_gates)
    B = a_ctx.shape[0]
    state0 = (jnp.zeros((B, M_HEADS, M_HEAD_DIM, M_HEAD_DIM), jnp.float32),
              jnp.zeros((B, M_HEADS, M_HEAD_DIM), jnp.float32),
              jnp.zeros((B, M_HEADS), jnp.float32))
    h_ctx, st_f, st_b = _mlstm_bidir(qc, kc, vc, gc, state0, state0)
    h_lat, _, _ = _mlstm_bidir(ql, kl, vl, gl, st_f, st_b)

    def merge(p, h, on_grid):
        pool_out = _pool_mixer(p[0], w_pool, pool_scale, on_grid)
        m_out = _mlstm_readout(h, p[4], mh_gain)
        y = jax.nn.sigmoid(p[5]) * pool_out + jax.nn.sigmoid(p[6]) * m_out
        return y @ w_out

    mix_lat = merge(p_lat, h_lat, True)
    mix_ctx = merge(p_ctx, h_ctx, False) if with_ctx_out else None
    return mix_lat, mix_ctx


def setup_inputs(seed: int = 0) -> dict:
    key = jax.random.key(seed)
    ks = jax.random.split(key, 24)

    def nrm(k, shape, scale):
        return jax.random.normal(k, shape, jnp.float32) * scale

    G = POOL_GROUP_DIM
    gi = nrm(ks[11], (DEPTH, 2, M_HEADS), 0.1)
    gf = jnp.linspace(3.0, 6.0, M_HEADS, dtype=jnp.float32) + nrm(ks[12], (DEPTH, 2, M_HEADS), 0.1)
    b_gates = jnp.stack([gi, gf], axis=2).reshape(DEPTH, N_GATE_COLS)
    return {
        "x": nrm(ks[0], (BATCH, SEQ, D_MODEL), 1.0),
        "c": nrm(ks[1], (BATCH, D_MODEL), 1.0),
        "ctx": nrm(ks[2], (BATCH, CTX_LEN, D_MODEL), 1.0),
        "c_ctx": nrm(ks[3], (D_MODEL,), 1.0),
        "norm_mix": 1.0 + nrm(ks[4], (DEPTH, D_MODEL), 0.02),
        "norm_ffn": 1.0 + nrm(ks[5], (DEPTH, D_MODEL), 0.02),
        "norm_final": 1.0 + nrm(ks[6], (D_MODEL,), 0.02),
        "w_ada": nrm(ks[7], (DEPTH, D_MODEL, 6 * D_MODEL), 0.5 * D_MODEL ** -0.5),
        "b_ada": nrm(ks[8], (DEPTH, 6 * D_MODEL), 0.01),
        "w_in": nrm(ks[9], (DEPTH, D_MODEL, IN_COLS), D_MODEL ** -0.5),
        "b_gates": b_gates,
        "conv_w": nrm(ks[13], (DEPTH, CONV_W, 2 * M_WIDTH), CONV_W ** -0.5),
        "conv_b": nrm(ks[14], (DEPTH, 2 * M_WIDTH), 0.01),
        "w_pool": nrm(ks[15], (DEPTH, POOL_GROUPS, G, G), G ** -0.5),
        "pool_scale": 1.0 + nrm(ks[16], (DEPTH, POOL_WIDTH), 0.1),
        "mh_gain": 1.0 + nrm(ks[17], (DEPTH, M_WIDTH), 0.02),
        "w_out": nrm(ks[18], (DEPTH, D_MODEL, D_MODEL), D_MODEL ** -0.5),
        "w_ffn_in": nrm(ks[19], (DEPTH, D_MODEL, 2 * D_FF), D_MODEL ** -0.5),
        "w_ffn_out": nrm(ks[20], (DEPTH, D_FF, D_MODEL), D_FF ** -0.5),
    }


def reference(x, c, ctx, c_ctx, norm_mix, norm_ffn, norm_final, w_ada, b_ada, w_in, b_gates, conv_w, conv_b,
              w_pool, pool_scale, mh_gain, w_out, w_ffn_in, w_ffn_out):
    xc = ctx
    for l in range(DEPTH):
        last = l == DEPTH - 1
        mod = jax.nn.silu(c) @ w_ada[l] + b_ada[l]
        mod_c = jax.nn.silu(c_ctx) @ w_ada[l] + b_ada[l]
        sh1, sc1, g1, sh2, sc2, g2 = jnp.split(mod[:, None, :], 6, axis=-1)
        csh1, csc1, cg1, csh2, csc2, cg2 = jnp.split(mod_c, 6)
        a_lat = _modulate(x, norm_mix[l], sh1, sc1)
        a_ctx = _modulate(xc, norm_mix[l], csh1, csc1)
        mix_lat, mix_ctx = _token_mixer(a_lat, a_ctx, w_in[l], b_gates[l], conv_w[l], conv_b[l], w_pool[l],
                                        pool_scale[l], mh_gain[l], w_out[l], not last)
        x = x + g1 * mix_lat
        x = x + g2 * _swiglu(_modulate(x, norm_ffn[l], sh2, sc2), w_ffn_in[l], w_ffn_out[l])
        if not last:
            xc = xc + cg1 * mix_ctx
            xc = xc + cg2 * _swiglu(_modulate(xc, norm_ffn[l], csh2, csc2), w_ffn_in[l], w_ffn_out[l])
    return _rmsnorm(x, norm_final)
```

```python
import functools

import numpy as np
import jax
import jax.numpy as jnp
from jax import lax
from jax.experimental import pallas as pl
from jax.experimental.pallas import tpu as pltpu

F32 = jnp.float32
BF16 = jnp.bfloat16

EPS = 1e-6
GRID_W = 64
POOL_WINDOWS = (2, 4, 8, 16)
M_HEADS = 4
CONV_W = 3
N_GATE_COLS = 4 * M_HEADS
GATE_LANES = 128
SCAN_CHUNK = 256
TOKEN_TILE = 512
HALO_ROWS = 16
VMEM_LIMIT = 60 * 1024 * 1024


def _resident(shape):
    nd = len(shape)
    return pl.BlockSpec(shape, lambda *_: (0,) * nd, pipeline_mode=pl.Buffered(1))


def _params(sem):
    return pltpu.CompilerParams(dimension_semantics=sem, vmem_limit_bytes=VMEM_LIMIT)


def _sigmoid(x):
    return 1.0 / (1.0 + jnp.exp(-x))


def _silu(x):
    return x * _sigmoid(x)


def _rms_scale(xf, d):
    return lax.rsqrt(jnp.sum(xf * xf, axis=-1, keepdims=True) * (1.0 / d) + EPS)


def _ada_kernel(c_ref, w_ref, b_ref, o_ref):
    s = _silu(c_ref[...])
    o_ref[...] = jnp.dot(s.astype(BF16), w_ref[...].astype(BF16), preferred_element_type=F32) + b_ref[...]


def _ada(c_rows, w_ada, b_ada):
    r, d = c_rows.shape
    n = w_ada.shape[1]
    return pl.pallas_call(
        _ada_kernel,
        grid=(n // d,),
        in_specs=[pl.BlockSpec((r, d), lambda j: (0, 0)),
                  pl.BlockSpec((d, d), lambda j: (0, j)),
                  pl.BlockSpec((1, d), lambda j: (0, j))],
        out_specs=pl.BlockSpec((r, d), lambda j: (0, j)),
        out_shape=jax.ShapeDtypeStruct((r, n), F32),
        compiler_params=_params(("arbitrary",)),
        name="ada",
    )(c_rows, w_ada, b_ada)


def _in_proj_kernel(kinds, q_scale, xp_ref, xc_ref, xn_ref, sh_ref, sc_ref, gain_ref, w_ref, wg_ref,
                    cw_ref, cb_ref, bg_ref, proj_ref, gates_ref, a_ref, qk_ref):
    i = pl.program_id(1)
    nt = pl.num_programs(1)
    t = xc_ref.shape[1]
    d = xc_ref.shape[2]
    gain = gain_ref[...]
    mult = 1.0 + sc_ref[0]
    shift = sh_ref[0]

    def norm_mod(xf):
        return (xf * _rms_scale(xf, d)) * gain * mult + shift

    prev = jnp.where(i > 0, norm_mod(xp_ref[0]), 0.0)
    nxt = jnp.where(i < nt - 1, norm_mod(xn_ref[0]), 0.0)
    zeros8 = jnp.zeros((HALO_ROWS - 8, d), F32)
    a_ref[0:HALO_ROWS, :] = jnp.concatenate([zeros8, prev], axis=0).astype(BF16)
    a_ref[HALO_ROWS:HALO_ROWS + t, :] = norm_mod(xc_ref[0]).astype(BF16)
    a_ref[HALO_ROWS + t:2 * HALO_ROWS + t, :] = jnp.concatenate([nxt, zeros8], axis=0).astype(BF16)

    for g, kind in enumerate(kinds):
        cols = slice(g * d, (g + 1) * d)
        if kind in ("q", "k"):
            qk_ref[...] = jnp.dot(a_ref[...], w_ref[:, cols], preferred_element_type=F32)
            c0 = 0 if kind == "q" else d
            cw = cw_ref[:, c0:c0 + d]
            y = (cb_ref[:, c0:c0 + d]
                 + qk_ref[HALO_ROWS - 1:HALO_ROWS - 1 + t, :] * cw[0:1]
                 + qk_ref[HALO_ROWS:HALO_ROWS + t, :] * cw[1:2]
                 + qk_ref[HALO_ROWS + 1:HALO_ROWS + 1 + t, :] * cw[2:3])
            y = _silu(y)
            if kind == "q":
                y = y * q_scale
        else:
            y = jnp.dot(a_ref[HALO_ROWS:HALO_ROWS + t, :], w_ref[:, cols], preferred_element_type=F32)
            if kind == "sig":
                y = _sigmoid(y)
        proj_ref[0, :, cols] = y.astype(BF16)

    gates_ref[0] = jnp.dot(a_ref[HALO_ROWS:HALO_ROWS + t, :], wg_ref[...],
                           preferred_element_type=F32) + bg_ref[...]


def _in_proj(x, mod, mod_cols, gain, w, wg, conv_w, conv_b, bg, kinds, q_scale):
    b, n, d = x.shape
    t = min(TOKEN_TILE, n)
    nt = n // t
    per_batch = mod.shape[0] > 1
    sh_col, sc_col = mod_cols

    def mod_map(col):
        return lambda bi, i: (bi if per_batch else 0, 0, col)

    t8 = t // 8
    last8 = n // 8 - 1
    in_specs = [
        pl.BlockSpec((1, 8, d), lambda bi, i: (bi, jnp.maximum(i * t8 - 1, 0), 0)),
        pl.BlockSpec((1, t, d), lambda bi, i: (bi, i, 0)),
        pl.BlockSpec((1, 8, d), lambda bi, i: (bi, jnp.minimum((i + 1) * t8, last8), 0)),
        pl.BlockSpec((1, 1, d), mod_map(sh_col)),
        pl.BlockSpec((1, 1, d), mod_map(sc_col)),
        _resident(gain.shape), _resident(w.shape), _resident(wg.shape),
        _resident(conv_w.shape), _resident(conv_b.shape), _resident(bg.shape),
    ]
    ncol = len(kinds) * d
    return pl.pallas_call(
        functools.partial(_in_proj_kernel, kinds, q_scale),
        grid=(b, nt),
        in_specs=in_specs,
        out_specs=[pl.BlockSpec((1, t, ncol), lambda bi, i: (bi, i, 0)),
                   pl.BlockSpec((1, t, GATE_LANES), lambda bi, i: (bi, i, 0))],
        out_shape=[jax.ShapeDtypeStruct((b, n, ncol), BF16),
                   jax.ShapeDtypeStruct((b, n, GATE_LANES), F32)],
        scratch_shapes=[pltpu.VMEM((t + 2 * HALO_ROWS, d), BF16),
                        pltpu.VMEM((t + 2 * HALO_ROWS, d), F32)],
        compiler_params=_params(("arbitrary", "arbitrary")),
        name="in_proj",
    )(x, x, x, mod, mod, gain, w, wg, conv_w, conv_b, bg)


def _split3(x):
    hi = x.astype(BF16)
    r1 = x - hi.astype(F32)
    mid = r1.astype(BF16)
    lo = (r1 - mid.astype(F32)).astype(BF16)
    return hi, mid, lo


def _gateprep_kernel(gates_ref, gcol_ref, grow_ref):
    x = gates_ref[0]
    l = x.shape[0]
    lane = lax.broadcasted_iota(jnp.int32, x.shape, 1)
    row = lax.broadcasted_iota(jnp.int32, x.shape, 0)
    fwd_lane = (lane >= 4) & (lane < 8)
    bwd_lane = (lane >= 12) & (lane < 16)

    logf = jnp.minimum(x, 0.0) - jnp.log1p(jnp.exp(-jnp.abs(x)))
    r = lax.broadcasted_iota(jnp.int32, (l, l), 0)
    c = lax.broadcasted_iota(jnp.int32, (l, l), 1)
    tril = (c <= r).astype(BF16)
    triu = (c >= r).astype(BF16)
    pre = jnp.zeros_like(x)
    suf = jnp.zeros_like(x)
    for part in _split3(logf):
        pre = pre + jnp.dot(tril, part, preferred_element_type=F32)
        suf = suf + jnp.dot(triu, part, preferred_element_type=F32)
    csum = jnp.where(fwd_lane, pre, suf)
    g = pltpu.roll(x, 4, axis=1) - csum

    neg = jnp.float32(-jnp.inf)
    mf = g
    mb = g
    s = 1
    while s < l:
        mf = jnp.maximum(mf, jnp.where(row >= s, pltpu.roll(mf, s, axis=0), neg))
        mb = jnp.maximum(mb, jnp.where(row < l - s, pltpu.roll(mb, l - s, axis=0), neg))
        s *= 2
    gmax = jnp.where(fwd_lane, mf, mb)

    used = fwd_lane | bwd_lane
    gz = jnp.where(used, g, 0.0)
    gcol_ref[0] = (gz + pltpu.roll(jnp.where(used, gmax, 0.0), 16, axis=1)
                   + pltpu.roll(jnp.where(used, csum, 0.0), 32, axis=1))
    grow_ref[0] = gz.T[0:N_GATE_COLS, :]


def _gateprep(gates, l):
    b, n, _ = gates.shape
    return pl.pallas_call(
        _gateprep_kernel,
        grid=(b, n // l),
        in_specs=[pl.BlockSpec((1, l, GATE_LANES), lambda bi, ci: (bi, ci, 0))],
        out_specs=[pl.BlockSpec((1, l, GATE_LANES), lambda bi, ci: (bi, ci, 0)),
                   pl.BlockSpec((1, N_GATE_COLS, l), lambda bi, ci: (bi, 0, ci))],
        out_shape=[jax.ShapeDtypeStruct((b, n, GATE_LANES), F32),
                   jax.ShapeDtypeStruct((b, N_GATE_COLS, n), F32)],
        compiler_params=_params(("arbitrary", "arbitrary")),
        name="gateprep",
    )(gates)


def _mlstm_kernel(backward, has_init, emit_h, emit_state, *refs):
    refs = list(refs)
    q_ref, k_ref, v_ref, gcol_ref, grow_ref = refs[:5]
    pos = 5
    if has_init:
        c0_ref, n0_ref, m0_ref = refs[pos:pos + 3]
        pos += 3
    if emit_h:
        h_ref = refs[pos]
        pos += 1
    if emit_state:
        co_ref, no_ref, mo_ref = refs[pos:pos + 3]
        pos += 3
    c_s, cb_s, n_s, m_s = refs[pos:pos + 4]

    j = pl.program_id(1)
    nj = pl.num_programs(1)
    l = q_ref.shape[1]
    dh = q_ref.shape[2] // M_HEADS

    @pl.when(j == 0)
    def _init():
        if has_init:
            c_s[...] = c0_ref[0]
            cb_s[...] = c0_ref[0].astype(BF16)
            n_s[...] = n0_ref[0]
            m_s[...] = m0_ref[0]
        else:
            c_s[...] = jnp.zeros_like(c_s)
            cb_s[...] = jnp.zeros_like(cb_s)
            n_s[...] = jnp.zeros_like(n_s)
            m_s[...] = jnp.zeros_like(m_s)

    r = lax.broadcasted_iota(jnp.int32, (l, l), 0)
    c = lax.broadcasted_iota(jnp.int32, (l, l), 1)
    visible = (c >= r) if backward else (c <= r)
    last = 0 if backward else l - 1
    gc = gcol_ref[0]
    neg = jnp.float32(-jnp.inf)

    for h in range(M_HEADS):
        lane = (12 if backward else 4) + h
        cols = slice(h * dh, (h + 1) * dh)
        q = q_ref[0, :, cols]
        k = k_ref[0, :, cols]
        v = v_ref[0, :, cols]
        g_col = gc[:, lane:lane + 1]
        gmax_col = gc[:, lane + 16:lane + 17]
        csum_col = gc[:, lane + 32:lane + 33]
        g_row = grow_ref[0, lane:lane + 1, :]
        m0 = m_s[h, :, 0:1]
        n0 = n_s[h]

        big_m = jnp.maximum(m0, gmax_col)
        m_last = big_m[last:last + 1, :]

        if emit_h:
            qk = lax.dot_general(q, k, (((1,), (1,)), ((), ())), preferred_element_type=F32)
            s_mat = qk * jnp.exp(jnp.where(visible, g_row - big_m, neg))
            sa = jnp.exp(m0 - big_m)
            num = (jnp.dot(s_mat.astype(BF16), v, preferred_element_type=F32)
                   + sa * jnp.dot(q, cb_s[h], preferred_element_type=F32))
            den = (jnp.sum(s_mat, axis=-1, keepdims=True)
                   + sa * jnp.sum(q.astype(F32) * n0, axis=-1, keepdims=True))
            floor = jnp.exp(-(csum_col + big_m))
            h_ref[0, :, cols] = num * (1.0 / jnp.maximum(jnp.abs(den), floor))

        ws = jnp.exp(g_col - m_last)
        s0 = jnp.exp(m0 - m_last)
        kw = k.astype(F32) * ws
        c_new = s0 * c_s[h] + lax.dot_general(kw.astype(BF16), v, (((0,), (0,)), ((), ())),
                                              preferred_element_type=F32)
        c_s[h] = c_new
        cb_s[h] = c_new.astype(BF16)
        n_s[h] = s0 * n0 + jnp.sum(kw, axis=0, keepdims=True)
        m_s[h] = jnp.broadcast_to(csum_col[last:last + 1, :] + m_last, m_s.shape[1:])

    if emit_state:
        @pl.when(j == nj - 1)
        def _emit():
            co_ref[0] = c_s[...]
            no_ref[0] = n_s[...]
            mo_ref[0] = m_s[...]


def _mlstm(proj, dm, qkv_cols, gcol, grow, init, backward, emit_h, emit_state):
    b, n, _ = proj.shape
    l = min(SCAN_CHUNK, n)
    nc = n // l
    dh = dm // M_HEADS

    def chunk(j):
        return nc - 1 - j if backward else j

    in_specs = [pl.BlockSpec((1, l, dm), (lambda col: (lambda bi, j: (bi, chunk(j), col)))(col))
                for col in qkv_cols]
    in_specs += [pl.BlockSpec((1, l, GATE_LANES), lambda bi, j: (bi, chunk(j), 0)),
                 pl.BlockSpec((1, N_GATE_COLS, l), lambda bi, j: (bi, 0, chunk(j)))]
    args = [proj, proj, proj, gcol, grow]
    state_specs = [pl.BlockSpec((1, M_HEADS, dh, dh), lambda bi, j: (bi, 0, 0, 0)),
                   pl.BlockSpec((1, M_HEADS, 1, dh), lambda bi, j: (bi, 0, 0, 0)),
                   pl.BlockSpec((1, M_HEADS, 1, GATE_LANES), lambda bi, j: (bi, 0, 0, 0))]
    state_shapes = [jax.ShapeDtypeStruct((b, M_HEADS, dh, dh), F32),
                    jax.ShapeDtypeStruct((b, M_HEADS, 1, dh), F32),
                    jax.ShapeDtypeStruct((b, M_HEADS, 1, GATE_LANES), F32)]
    if init is not None:
        in_specs += state_specs
        args += list(init)
    out_specs, out_shape = [], []
    if emit_h:
        out_specs.append(pl.BlockSpec((1, l, dm), lambda bi, j: (bi, chunk(j), 0)))
        out_shape.append(jax.ShapeDtypeStruct((b, n, dm), F32))
    if emit_state:
        out_specs += state_specs
        out_shape += state_shapes
    return pl.pallas_call(
        functools.partial(_mlstm_kernel, backward, init is not None, emit_h, emit_state),
        grid=(b, nc),
        in_specs=in_specs,
        out_specs=out_specs,
        out_shape=out_shape,
        scratch_shapes=[pltpu.VMEM((M_HEADS, dh, dh), F32), pltpu.VMEM((M_HEADS, dh, dh), BF16),
                        pltpu.VMEM((M_HEADS, 1, dh), F32), pltpu.VMEM((M_HEADS, 1, GATE_LANES), F32)],
        compiler_params=_params(("arbitrary", "arbitrary")),
        name="mlstm_bwd" if backward else "mlstm_fwd",
    )(*args)


BAND = 256


def _band_matrices():
    t = np.arange(BAND)
    same_row = (t[:, None] // GRID_W) == (t[None, :] // GRID_W)
    ct, cs = t[:, None] % GRID_W, t[None, :] % GRID_W
    mats = [same_row & (cs >= ct - w // 2) & (cs <= ct + w // 2 - 1) for w in POOL_WINDOWS]
    return jnp.asarray(np.stack(mats).astype(np.float32), dtype=BF16)


def _pool_kernel(up_ref, uc_ref, un_ref, sg_ref, band_ref, wp_ref, ps_ref, y_ref, cs_ref):
    i = pl.program_id(1)
    nt = pl.num_programs(1)
    t = uc_ref.shape[1]
    gdim = wp_ref.shape[1]
    rt = t // GRID_W
    shift = GRID_W.bit_length() - 1
    tok = lax.broadcasted_iota(jnp.int32, (t, 1), 0)
    row = i * rt + lax.shift_right_logical(tok, shift)
    col = tok & (GRID_W - 1)
    prev_ok = i > 0
    next_ok = i < nt - 1

    for g, w in enumerate(POOL_WINDOWS):
        hw = w // 2
        cols = slice(g * gdim, (g + 1) * gdim)
        band = band_ref[g]
        lo = (rt - hw) * GRID_W // BAND * BAND
        hi = -(-((2 * rt + hw - 1) * GRID_W) // BAND) * BAND
        for b0 in range(lo, hi, BAND):
            seg, off = divmod(b0, t)
            if seg == 0:
                blk = jnp.where(prev_ok, up_ref[0, off:off + BAND, cols], 0)
            elif seg == 1:
                blk = uc_ref[0, off:off + BAND, cols]
            else:
                blk = jnp.where(next_ok, un_ref[0, off:off + BAND, cols], 0)
            cs_ref[b0:b0 + BAND, :] = jnp.dot(band, blk, preferred_element_type=F32)
        acc = cs_ref[(rt - hw) * GRID_W:(rt - hw) * GRID_W + t, :]
        for jr in range(-hw + 1, hw):
            acc = acc + cs_ref[(rt + jr) * GRID_W:(rt + jr) * GRID_W + t, :]
        cnt_r = jnp.minimum(row + hw, nt * rt) - jnp.maximum(row - hw, 0)
        cnt_c = jnp.minimum(col + hw, GRID_W) - jnp.maximum(col - hw, 0)
        inv = 1.0 / (cnt_r * cnt_c).astype(F32)
        delta = acc * inv - uc_ref[0, :, cols].astype(F32)
        po = jnp.dot(delta.astype(BF16), wp_ref[g], preferred_element_type=F32)
        y_ref[0, :, cols] = (sg_ref[0, :, cols].astype(F32) * (po * ps_ref[:, cols])).astype(BF16)


def _pool(proj, dm, u_col, sg_col, bands, w_pool, pool_scale):
    b, n, _ = proj.shape
    t = TOKEN_TILE
    assert n % t == 0 and t % BAND == 0 and BAND % GRID_W == 0 and GRID_W & (GRID_W - 1) == 0
    assert t // GRID_W >= max(POOL_WINDOWS) // 2
    nt = n // t
    gdim = dm // len(POOL_WINDOWS)
    return pl.pallas_call(
        _pool_kernel,
        grid=(b, nt),
        in_specs=[pl.BlockSpec((1, t, dm), lambda bi, i: (bi, jnp.maximum(i - 1, 0), u_col)),
                  pl.BlockSpec((1, t, dm), lambda bi, i: (bi, i, u_col)),
                  pl.BlockSpec((1, t, dm), lambda bi, i: (bi, jnp.minimum(i + 1, nt - 1), u_col)),
                  pl.BlockSpec((1, t, dm), lambda bi, i: (bi, i, sg_col)),
                  _resident(bands.shape), _resident(w_pool.shape), _resident(pool_scale.shape)],
        out_specs=pl.BlockSpec((1, t, dm), lambda bi, i: (bi, i, 0)),
        out_shape=jax.ShapeDtypeStruct((b, n, dm), BF16),
        scratch_shapes=[pltpu.VMEM((3 * t, gdim), F32)],
        compiler_params=_params(("arbitrary", "arbitrary")),
        name="pool",
    )(proj, proj, proj, proj, bands, w_pool, pool_scale)


FFN_CHUNK = 256


def _tail_kernel(x_ref, hf_ref, hb_ref, yp_ref, so_ref, sgm_ref, g1_ref, sh2_ref, sc2_ref, g2_ref,
                 mhg_ref, wo_ref, nf_ref, wi_ref, wo2_ref, nfin_ref, o_ref):
    d = x_ref.shape[2]
    dh = d // M_HEADS
    f = wo2_ref.shape[0]

    hs = hf_ref[0] + hb_ref[0]
    parts = []
    for h in range(M_HEADS):
        hh = hs[:, h * dh:(h + 1) * dh]
        parts.append(hh * _rms_scale(hh, dh))
    m_out = jnp.concatenate(parts, axis=-1) * mhg_ref[...] * so_ref[0].astype(F32)
    y = yp_ref[0].astype(F32) + sgm_ref[0].astype(F32) * m_out
    mix = jnp.dot(y.astype(BF16), wo_ref[...], preferred_element_type=F32)
    x1 = x_ref[0] + g1_ref[0] * mix

    a2 = ((x1 * _rms_scale(x1, d)) * nf_ref[...] * (1.0 + sc2_ref[0]) + sh2_ref[0]).astype(BF16)
    acc = jnp.zeros_like(x1)
    for c0 in range(0, f, FFN_CHUNK):
        gate = jnp.dot(a2, wi_ref[:, c0:c0 + FFN_CHUNK], preferred_element_type=F32)
        up = jnp.dot(a2, wi_ref[:, f + c0:f + c0 + FFN_CHUNK], preferred_element_type=F32)
        act = (_silu(gate) * up).astype(BF16)
        acc = acc + jnp.dot(act, wo2_ref[c0:c0 + FFN_CHUNK, :], preferred_element_type=F32)
    x2 = x1 + g2_ref[0] * acc
    o_ref[0] = (x2 * _rms_scale(x2, d)) * nfin_ref[...]


def _tail(x, hf, hb, ypool, proj, so_col, sgm_col, mod, mh_gain, w_out, norm_ffn, w_ffn_in, w_ffn_out,
          norm_final):
    b, n, d = x.shape
    t = min(TOKEN_TILE, n)
    assert w_ffn_out.shape[0] % FFN_CHUNK == 0

    def tile(col=0):
        return pl.BlockSpec((1, t, d), lambda bi, i: (bi, i, col))

    def mod_spec(col):
        return pl.BlockSpec((1, 1, d), lambda bi, i: (bi, 0, col))

    return pl.pallas_call(
        _tail_kernel,
        grid=(b, n // t),
        in_specs=[tile(), tile(), tile(), tile(), tile(so_col), tile(sgm_col),
                  mod_spec(2), mod_spec(3), mod_spec(4), mod_spec(5),
                  _resident(mh_gain.shape), _resident(w_out.shape), _resident(norm_ffn.shape),
                  _resident(w_ffn_in.shape), _resident(w_ffn_out.shape), _resident(norm_final.shape)],
        out_specs=tile(),
        out_shape=jax.ShapeDtypeStruct((b, n, d), F32),
        compiler_params=_params(("arbitrary", "arbitrary")),
        name="tail",
    )(x, hf, hb, ypool, proj, proj, mod, mod, mod, mod, mh_gain, w_out, norm_ffn, w_ffn_in, w_ffn_out,
      norm_final)


def kernel(x, c, ctx, c_ctx, norm_mix, norm_ffn, norm_final, w_ada, b_ada, w_in, b_gates, conv_w, conv_b,
           w_pool, pool_scale, mh_gain, w_out, w_ffn_in, w_ffn_out):
    assert w_ada.shape[0] == 1, "single-layer stack only"
    b, n, d = x.shape
    assert d % (128 * M_HEADS) == 0 and w_in.shape[2] == 7 * d + N_GATE_COLS
    dh = d // M_HEADS

    pad_rows = -(b + 1) % 8
    c_rows = jnp.concatenate([c, c_ctx[None], jnp.zeros((pad_rows, d), F32)], axis=0)
    mod_all = _ada(c_rows, w_ada[0], b_ada)
    mod = mod_all[:b, None, :]
    mod_ctx = mod_all[b:b + 1, None, :]

    w_main = w_in[0, :, :7 * d].astype(BF16)
    w_gate = jnp.pad(w_in[0, :, 7 * d:], ((0, 0), (0, GATE_LANES - N_GATE_COLS))).astype(BF16)
    b_gate = jnp.pad(b_gates, ((0, 0), (0, GATE_LANES - N_GATE_COLS)))
    q_scale = dh ** -0.5

    lat_kinds = ("plain", "q", "k", "plain", "sig", "sig", "sig")
    proj, gates = _in_proj(x, mod, (0, 1), norm_mix, w_main, w_gate, conv_w[0], conv_b, b_gate,
                           lat_kinds, q_scale)
    proj_c, gates_c = _in_proj(ctx, mod_ctx, (0, 1), norm_mix, w_main[:, d:4 * d], w_gate, conv_w[0], conv_b,
                               b_gate, ("q", "k", "plain"), q_scale)

    gcol, grow = _gateprep(gates, min(SCAN_CHUNK, n))
    gcol_c, grow_c = _gateprep(gates_c, min(SCAN_CHUNK, ctx.shape[1]))

    st_f = _mlstm(proj_c, d, (0, 1, 2), gcol_c, grow_c, None, False, False, True)
    st_b = _mlstm(proj_c, d, (0, 1, 2), gcol_c, grow_c, None, True, False, True)
    (h_f,) = _mlstm(proj, d, (1, 2, 3), gcol, grow, st_f, False, True, False)
    (h_b,) = _mlstm(proj, d, (1, 2, 3), gcol, grow, st_b, True, True, False)

    ypool = _pool(proj, d, 0, 5, _band_matrices(), w_pool[0].astype(BF16), pool_scale)
    return _tail(x, h_f, h_b, ypool, proj, 4, 6, mod, mh_gain, w_out[0].astype(BF16), norm_ffn,
                 w_ffn_in[0].astype(BF16), w_ffn_out[0].astype(BF16), norm_final[None])
```

```python
import functools
import math

import numpy as np
import jax
import jax.numpy as jnp
from jax import lax
from jax.experimental import pallas as pl
from jax.experimental.pallas import tpu as pltpu

F32 = jnp.float32
BF16 = jnp.bfloat16

EPS = 1e-6
LOG2E = math.log2(math.e)
GRID_W = 64
POOL_WINDOWS = (2, 4, 8, 16)
M_HEADS = 4
CONV_W = 3
N_GATE_COLS = 4 * M_HEADS
GATE_LANES = 128
SCAN_CHUNK = 256
TOKEN_TILE = 512
HALO_ROWS = 16
VMEM_LIMIT = 60 * 1024 * 1024


def _resident(shape):
    nd = len(shape)
    return pl.BlockSpec(shape, lambda *_: (0,) * nd, pipeline_mode=pl.Buffered(1))


def _params(sem):
    return pltpu.CompilerParams(dimension_semantics=sem, vmem_limit_bytes=VMEM_LIMIT)


def _sigmoid(x):
    return 1.0 / (1.0 + jnp.exp(-x))


def _silu(x):
    return x * _sigmoid(x)


def _rms_scale(xf, d):
    return lax.rsqrt(jnp.sum(xf * xf, axis=-1, keepdims=True) * (1.0 / d) + EPS)


def _ada_kernel(c_ref, w_ref, b_ref, o_ref):
    s = _silu(c_ref[...])
    o_ref[...] = jnp.dot(s.astype(BF16), w_ref[...].astype(BF16), preferred_element_type=F32) + b_ref[...]


def _ada(c_rows, w_ada, b_ada):
    r, d = c_rows.shape
    n = w_ada.shape[1]
    return pl.pallas_call(
        _ada_kernel,
        grid=(n // d,),
        in_specs=[pl.BlockSpec((r, d), lambda j: (0, 0)),
                  pl.BlockSpec((d, d), lambda j: (0, j)),
                  pl.BlockSpec((1, d), lambda j: (0, j))],
        out_specs=pl.BlockSpec((r, d), lambda j: (0, j)),
        out_shape=jax.ShapeDtypeStruct((r, n), F32),
        compiler_params=_params(("arbitrary",)),
        name="ada",
    )(c_rows, w_ada, b_ada)


def _in_proj_kernel(kinds, q_scale, xp_ref, xc_ref, xn_ref, sh_ref, sc_ref, gain_ref, w_ref, wg_ref,
                    cw_ref, cb_ref, bg_ref, proj_ref, kt_ref, gates_ref, a_ref, qk_ref):
    i = pl.program_id(1)
    nt = pl.num_programs(1)
    t = xc_ref.shape[1]
    d = xc_ref.shape[2]
    gain = gain_ref[...]
    mult = 1.0 + sc_ref[0]
    shift = sh_ref[0]

    def norm_mod(xf):
        return (xf * _rms_scale(xf, d)) * gain * mult + shift

    prev = jnp.where(i > 0, norm_mod(xp_ref[0]), 0.0)
    nxt = jnp.where(i < nt - 1, norm_mod(xn_ref[0]), 0.0)
    zeros8 = jnp.zeros((HALO_ROWS - 8, d), F32)
    a_ref[0:HALO_ROWS, :] = jnp.concatenate([zeros8, prev], axis=0).astype(BF16)
    a_ref[HALO_ROWS:HALO_ROWS + t, :] = norm_mod(xc_ref[0]).astype(BF16)
    a_ref[HALO_ROWS + t:2 * HALO_ROWS + t, :] = jnp.concatenate([nxt, zeros8], axis=0).astype(BF16)

    out_col = 0
    for g, kind in enumerate(kinds):
        cols = slice(g * d, (g + 1) * d)
        if kind in ("q", "k"):
            qk_ref[...] = jnp.dot(a_ref[...], w_ref[:, cols], preferred_element_type=F32)
            c0 = 0 if kind == "q" else d
            cw = cw_ref[:, c0:c0 + d]
            y = (cb_ref[:, c0:c0 + d]
                 + qk_ref[HALO_ROWS - 1:HALO_ROWS - 1 + t, :] * cw[0:1]
                 + qk_ref[HALO_ROWS:HALO_ROWS + t, :] * cw[1:2]
                 + qk_ref[HALO_ROWS + 1:HALO_ROWS + 1 + t, :] * cw[2:3])
            y = _silu(y)
            if kind == "q":
                y = y * q_scale
        else:
            y = jnp.dot(a_ref[HALO_ROWS:HALO_ROWS + t, :], w_ref[:, cols], preferred_element_type=F32)
            if kind == "sig":
                y = _sigmoid(y)
        if kind == "k":
            kt_ref[0] = y.T.astype(BF16)
        else:
            proj_ref[0, :, out_col * d:(out_col + 1) * d] = y.astype(BF16)
            out_col += 1

    gates_ref[0] = jnp.dot(a_ref[HALO_ROWS:HALO_ROWS + t, :], wg_ref[...],
                           preferred_element_type=F32) + bg_ref[...]


def _in_proj(x, mod, mod_cols, gain, w, wg, conv_w, conv_b, bg, kinds, q_scale):
    b, n, d = x.shape
    t = min(TOKEN_TILE, n)
    nt = n // t
    per_batch = mod.shape[0] > 1
    sh_col, sc_col = mod_cols

    def mod_map(col):
        return lambda bi, i: (bi if per_batch else 0, 0, col)

    t8 = t // 8
    last8 = n // 8 - 1
    in_specs = [
        pl.BlockSpec((1, 8, d), lambda bi, i: (bi, jnp.maximum(i * t8 - 1, 0), 0)),
        pl.BlockSpec((1, t, d), lambda bi, i: (bi, i, 0)),
        pl.BlockSpec((1, 8, d), lambda bi, i: (bi, jnp.minimum((i + 1) * t8, last8), 0)),
        pl.BlockSpec((1, 1, d), mod_map(sh_col)),
        pl.BlockSpec((1, 1, d), mod_map(sc_col)),
        _resident(gain.shape), _resident(w.shape), _resident(wg.shape),
        _resident(conv_w.shape), _resident(conv_b.shape), _resident(bg.shape),
    ]
    assert kinds.count("k") == 1
    ncol = (len(kinds) - 1) * d
    return pl.pallas_call(
        functools.partial(_in_proj_kernel, kinds, q_scale),
        grid=(b, nt),
        in_specs=in_specs,
        out_specs=[pl.BlockSpec((1, t, ncol), lambda bi, i: (bi, i, 0)),
                   pl.BlockSpec((1, d, t), lambda bi, i: (bi, 0, i)),
                   pl.BlockSpec((1, t, GATE_LANES), lambda bi, i: (bi, i, 0))],
        out_shape=[jax.ShapeDtypeStruct((b, n, ncol), BF16),
                   jax.ShapeDtypeStruct((b, d, n), BF16),
                   jax.ShapeDtypeStruct((b, n, GATE_LANES), F32)],
        scratch_shapes=[pltpu.VMEM((t + 2 * HALO_ROWS, d), BF16),
                        pltpu.VMEM((t + 2 * HALO_ROWS, d), F32)],
        compiler_params=_params(("arbitrary", "arbitrary")),
        name="in_proj",
    )(x, x, x, mod, mod, gain, w, wg, conv_w, conv_b, bg)


GROW_ROWS = 2 * N_GATE_COLS


def _split3(x):
    hi = x.astype(BF16)
    r1 = x - hi.astype(F32)
    mid = r1.astype(BF16)
    lo = (r1 - mid.astype(F32)).astype(BF16)
    return hi, mid, lo


def _gateprep_kernel(backward, has_init, *refs):
    refs = list(refs)
    gates_ref = refs.pop(0)
    m_init_ref = refs.pop(0) if has_init else None
    gcol_ref, grow_ref, tab_ref, m_out_ref, m_s = refs

    j = pl.program_id(1)
    nj = pl.num_programs(1)

    @pl.when(j == 0)
    def _init():
        m_s[...] = m_init_ref[0] if has_init else jnp.zeros_like(m_s)

    x = gates_ref[0]
    l = x.shape[0]
    lane = lax.broadcasted_iota(jnp.int32, x.shape, 1)
    row = lax.broadcasted_iota(jnp.int32, x.shape, 0)
    f0 = 12 if backward else 4
    used = (lane >= f0) & (lane < f0 + M_HEADS)
    last = 0 if backward else l - 1

    logf = jnp.minimum(x, 0.0) - jnp.log1p(jnp.exp(-jnp.abs(x)))
    r = lax.broadcasted_iota(jnp.int32, (l, l), 0)
    c = lax.broadcasted_iota(jnp.int32, (l, l), 1)
    tri = ((c >= r) if backward else (c <= r)).astype(BF16)
    csum = jnp.zeros_like(x)
    for part in _split3(logf):
        csum = csum + jnp.dot(tri, part, preferred_element_type=F32)
    g = pltpu.roll(x, 4, axis=1) - csum

    neg = jnp.float32(-jnp.inf)
    gmax = g
    s = 1
    while s < l:
        if backward:
            gmax = jnp.maximum(gmax, jnp.where(row < l - s, pltpu.roll(gmax, l - s, axis=0), neg))
        else:
            gmax = jnp.maximum(gmax, jnp.where(row >= s, pltpu.roll(gmax, s, axis=0), neg))
        s *= 2

    m0 = m_s[0:1, :]
    big_m = jnp.maximum(m0, gmax)
    m_last = big_m[last:last + 1, :]
    m_new = csum[last:last + 1, :] + m_last
    floor = jnp.exp(-(csum + big_m))
    ws = jnp.exp(g - m_last)
    s0 = jnp.exp(m0 - m_last)

    def keep(v):
        return jnp.where(used, v, 0.0)

    gcol_ref[0] = keep(big_m * LOG2E) + pltpu.roll(keep(floor), N_GATE_COLS, axis=1)
    packed = keep(g * LOG2E) + pltpu.roll(keep(ws), N_GATE_COLS, axis=1)
    grow_ref[0] = packed.T[0:GROW_ROWS, :]
    row8 = lax.broadcasted_iota(jnp.int32, (8, GATE_LANES), 0)
    tab_ref[0, 0] = jnp.where(row8 == 0, m0 * LOG2E, jnp.where(row8 == 1, s0, 0.0))
    m_s[...] = jnp.broadcast_to(m_new, m_s.shape)
    m_out_ref[0] = jnp.broadcast_to(m_new, m_s.shape)


def _gateprep(gates, l, backward, m_init):
    b, n, _ = gates.shape
    nc = n // l

    def chunk(j):
        return nc - 1 - j if backward else j

    in_specs = [pl.BlockSpec((1, l, GATE_LANES), lambda bi, j: (bi, chunk(j), 0))]
    args = [gates]
    if m_init is not None:
        in_specs.append(pl.BlockSpec((1, 8, GATE_LANES), lambda bi, j: (bi, 0, 0)))
        args.append(m_init)
    return pl.pallas_call(
        functools.partial(_gateprep_kernel, backward, m_init is not None),
        grid=(b, nc),
        in_specs=in_specs,
        out_specs=[pl.BlockSpec((1, l, GATE_LANES), lambda bi, j: (bi, chunk(j), 0)),
                   pl.BlockSpec((1, GROW_ROWS, l), lambda bi, j: (bi, 0, chunk(j))),
                   pl.BlockSpec((1, 1, 8, GATE_LANES), lambda bi, j: (bi, chunk(j), 0, 0)),
                   pl.BlockSpec((1, 8, GATE_LANES), lambda bi, j: (bi, 0, 0))],
        out_shape=[jax.ShapeDtypeStruct((b, n, GATE_LANES), F32),
                   jax.ShapeDtypeStruct((b, GROW_ROWS, n), F32),
                   jax.ShapeDtypeStruct((b, nc, 8, GATE_LANES), F32),
                   jax.ShapeDtypeStruct((b, 8, GATE_LANES), F32)],
        scratch_shapes=[pltpu.VMEM((8, GATE_LANES), F32)],
        compiler_params=_params(("arbitrary", "arbitrary")),
        name="gateprep_bwd" if backward else "gateprep_fwd",
    )(*args)


def _mlstm_kernel(backward, has_init, emit_h, emit_state, *refs):
    refs = list(refs)
    q_ref, kt_ref, v_ref, gcol_ref, grow_ref, tab_ref = refs[:6]
    pos = 6
    if has_init:
        c0_ref, n0_ref = refs[pos:pos + 2]
        pos += 2
    if emit_h:
        h_ref = refs[pos]
        pos += 1
    if emit_state:
        co_ref, no_ref = refs[pos:pos + 2]
        pos += 2
    c_s, cb_s, n_s, nb_s = refs[pos:pos + 4]

    j = pl.program_id(1)
    nj = pl.num_programs(1)
    l = q_ref.shape[1]
    dh = q_ref.shape[2] // M_HEADS

    @pl.when(j == 0)
    def _init():
        if has_init:
            c_s[...] = c0_ref[0]
            cb_s[...] = c0_ref[0].astype(BF16)
            n_s[...] = n0_ref[0]
            nb_s[...] = n0_ref[0].astype(BF16)
        else:
            c_s[...] = jnp.zeros_like(c_s)
            cb_s[...] = jnp.zeros_like(cb_s)
            n_s[...] = jnp.zeros_like(n_s)
            nb_s[...] = jnp.zeros_like(nb_s)

    r = lax.broadcasted_iota(jnp.int32, (l, l), 0)
    c = lax.broadcasted_iota(jnp.int32, (l, l), 1)
    visible = (c >= r) if backward else (c <= r)
    neg = jnp.float32(-jnp.inf)
    ones_rhs = jnp.ones((l, GATE_LANES), BF16)
    gc = gcol_ref[0]

    for h in range(M_HEADS):
        lane = (12 if backward else 4) + h
        cols = slice(h * dh, (h + 1) * dh)
        q = q_ref[0, :, cols]
        kt = kt_ref[0, cols, :]
        v = v_ref[0, :, cols]
        g_row = grow_ref[0, lane:lane + 1, :]
        ws_row = grow_ref[0, lane + N_GATE_COLS:lane + N_GATE_COLS + 1, :]
        m0 = tab_ref[0, 0, 0:1, lane:lane + 1]
        s0 = tab_ref[0, 0, 1:2, lane:lane + 1]

        if emit_h:
            m_col = gc[:, lane:lane + 1]
            floor = jnp.broadcast_to(gc[:, lane + N_GATE_COLS:lane + N_GATE_COLS + 1], (l, GATE_LANES))
            qk = jnp.dot(q, kt, preferred_element_type=F32)
            s_mat = (qk * jnp.exp2(jnp.where(visible, g_row - m_col, neg))).astype(BF16)
            sa = jnp.exp2(m0 - jnp.broadcast_to(m_col, (l, GATE_LANES)))
            num = (jnp.dot(s_mat, v, preferred_element_type=F32)
                   + jnp.concatenate([sa] * (dh // GATE_LANES), axis=1)
                   * jnp.dot(q, cb_s[h], preferred_element_type=F32))
            den = (jnp.dot(s_mat, ones_rhs, preferred_element_type=F32)
                   + sa * jnp.dot(q, nb_s[h], preferred_element_type=F32))
            rcp = 1.0 / jnp.maximum(jnp.abs(den), floor)
            h_ref[0, :, cols] = num * jnp.concatenate([rcp] * (dh // GATE_LANES), axis=1)

        ktw = (kt.astype(F32) * ws_row).astype(BF16)
        c_new = s0 * c_s[h] + jnp.dot(ktw, v, preferred_element_type=F32)
        n_new = s0 * n_s[h] + jnp.dot(ktw, ones_rhs, preferred_element_type=F32)
        c_s[h] = c_new
        cb_s[h] = c_new.astype(BF16)
        n_s[h] = n_new
        nb_s[h] = n_new.astype(BF16)

    if emit_state:
        @pl.when(j == nj - 1)
        def _emit():
            co_ref[0] = c_s[...]
            no_ref[0] = n_s[...]


def _mlstm(proj, kt, dm, q_col, v_col, gate_info, init, backward, emit_h, emit_state):
    gcol, grow, tab = gate_info
    b, n, _ = proj.shape
    l = n // tab.shape[1]
    nc = n // l
    dh = dm // M_HEADS

    def chunk(j):
        return nc - 1 - j if backward else j

    in_specs = [pl.BlockSpec((1, l, dm), lambda bi, j: (bi, chunk(j), q_col)),
                pl.BlockSpec((1, dm, l), lambda bi, j: (bi, 0, chunk(j))),
                pl.BlockSpec((1, l, dm), lambda bi, j: (bi, chunk(j), v_col)),
                pl.BlockSpec((1, l, GATE_LANES), lambda bi, j: (bi, chunk(j), 0)),
                pl.BlockSpec((1, GROW_ROWS, l), lambda bi, j: (bi, 0, chunk(j))),
                pl.BlockSpec((1, 1, 8, GATE_LANES), lambda bi, j: (bi, chunk(j), 0, 0))]
    args = [proj, kt, proj, gcol, grow, tab]
    state_specs = [pl.BlockSpec((1, M_HEADS, dh, dh), lambda bi, j: (bi, 0, 0, 0)),
                   pl.BlockSpec((1, M_HEADS, dh, GATE_LANES), lambda bi, j: (bi, 0, 0, 0))]
    state_shapes = [jax.ShapeDtypeStruct((b, M_HEADS, dh, dh), F32),
                    jax.ShapeDtypeStruct((b, M_HEADS, dh, GATE_LANES), F32)]
    if init is not None:
        in_specs += state_specs
        args += list(init)
    out_specs, out_shape = [], []
    if emit_h:
        out_specs.append(pl.BlockSpec((1, l, dm), lambda bi, j: (bi, chunk(j), 0)))
        out_shape.append(jax.ShapeDtypeStruct((b, n, dm), F32))
    if emit_state:
        out_specs += state_specs
        out_shape += state_shapes
    return pl.pallas_call(
        functools.partial(_mlstm_kernel, backward, init is not None, emit_h, emit_state),
        grid=(b, nc),
        in_specs=in_specs,
        out_specs=out_specs,
        out_shape=out_shape,
        scratch_shapes=[pltpu.VMEM((M_HEADS, dh, dh), F32), pltpu.VMEM((M_HEADS, dh, dh), BF16),
                        pltpu.VMEM((M_HEADS, dh, GATE_LANES), F32),
                        pltpu.VMEM((M_HEADS, dh, GATE_LANES), BF16)],
        compiler_params=_params(("arbitrary", "arbitrary")),
        name="mlstm_bwd" if backward else "mlstm_fwd",
    )(*args)


BAND = 256


def _band_matrices():
    t = np.arange(BAND)
    same_row = (t[:, None] // GRID_W) == (t[None, :] // GRID_W)
    ct, cs = t[:, None] % GRID_W, t[None, :] % GRID_W
    mats = [same_row & (cs >= ct - w // 2) & (cs <= ct + w // 2 - 1) for w in POOL_WINDOWS]
    return jnp.asarray(np.stack(mats).astype(np.float32), dtype=BF16)


def _pool_kernel(up_ref, uc_ref, un_ref, sg_ref, band_ref, wp_ref, ps_ref, y_ref, cs_ref):
    i = pl.program_id(1)
    nt = pl.num_programs(1)
    t = uc_ref.shape[1]
    gdim = wp_ref.shape[1]
    rt = t // GRID_W
    shift = GRID_W.bit_length() - 1
    tok = lax.broadcasted_iota(jnp.int32, (t, 1), 0)
    row = i * rt + lax.shift_right_logical(tok, shift)
    col = tok & (GRID_W - 1)
    prev_ok = i > 0
    next_ok = i < nt - 1

    for g, w in enumerate(POOL_WINDOWS):
        hw = w // 2
        cols = slice(g * gdim, (g + 1) * gdim)
        band = band_ref[g]
        lo = (rt - hw) * GRID_W // BAND * BAND
        hi = -(-((2 * rt + hw - 1) * GRID_W) // BAND) * BAND
        for b0 in range(lo, hi, BAND):
            seg, off = divmod(b0, t)
            if seg == 0:
                blk = jnp.where(prev_ok, up_ref[0, off:off + BAND, cols], 0)
            elif seg == 1:
                blk = uc_ref[0, off:off + BAND, cols]
            else:
                blk = jnp.where(next_ok, un_ref[0, off:off + BAND, cols], 0)
            cs_ref[b0:b0 + BAND, :] = jnp.dot(band, blk, preferred_element_type=F32)
        acc = cs_ref[(rt - hw) * GRID_W:(rt - hw) * GRID_W + t, :]
        for jr in range(-hw + 1, hw):
            acc = acc + cs_ref[(rt + jr) * GRID_W:(rt + jr) * GRID_W + t, :]
        cnt_r = jnp.minimum(row + hw, nt * rt) - jnp.maximum(row - hw, 0)
        cnt_c = jnp.minimum(col + hw, GRID_W) - jnp.maximum(col - hw, 0)
        inv = 1.0 / (cnt_r * cnt_c).astype(F32)
        delta = acc * inv - uc_ref[0, :, cols].astype(F32)
        po = jnp.dot(delta.astype(BF16), wp_ref[g], preferred_element_type=F32)
        y_ref[0, :, cols] = (sg_ref[0, :, cols].astype(F32) * (po * ps_ref[:, cols])).astype(BF16)


def _pool(proj, dm, u_col, sg_col, bands, w_pool, pool_scale):
    b, n, _ = proj.shape
    t = TOKEN_TILE
    assert n % t == 0 and t % BAND == 0 and BAND % GRID_W == 0 and GRID_W & (GRID_W - 1) == 0
    assert t // GRID_W >= max(POOL_WINDOWS) // 2
    nt = n // t
    gdim = dm // len(POOL_WINDOWS)
    return pl.pallas_call(
        _pool_kernel,
        grid=(b, nt),
        in_specs=[pl.BlockSpec((1, t, dm), lambda bi, i: (bi, jnp.maximum(i - 1, 0), u_col)),
                  pl.BlockSpec((1, t, dm), lambda bi, i: (bi, i, u_col)),
                  pl.BlockSpec((1, t, dm), lambda bi, i: (bi, jnp.minimum(i + 1, nt - 1), u_col)),
                  pl.BlockSpec((1, t, dm), lambda bi, i: (bi, i, sg_col)),
                  _resident(bands.shape), _resident(w_pool.shape), _resident(pool_scale.shape)],
        out_specs=pl.BlockSpec((1, t, dm), lambda bi, i: (bi, i, 0)),
        out_shape=jax.ShapeDtypeStruct((b, n, dm), BF16),
        scratch_shapes=[pltpu.VMEM((3 * t, gdim), F32)],
        compiler_params=_params(("arbitrary", "arbitrary")),
        name="pool",
    )(proj, proj, proj, proj, bands, w_pool, pool_scale)


FFN_CHUNK = 256


def _tail_kernel(x_ref, hf_ref, hb_ref, yp_ref, so_ref, sgm_ref, g1_ref, sh2_ref, sc2_ref, g2_ref,
                 mhg_ref, wo_ref, nf_ref, wi_ref, wo2_ref, nfin_ref, o_ref):
    d = x_ref.shape[2]
    dh = d // M_HEADS
    f = wo2_ref.shape[0]

    hs = hf_ref[0] + hb_ref[0]
    parts = []
    for h in range(M_HEADS):
        hh = hs[:, h * dh:(h + 1) * dh]
        parts.append(hh * _rms_scale(hh, dh))
    m_out = jnp.concatenate(parts, axis=-1) * mhg_ref[...] * so_ref[0].astype(F32)
    y = yp_ref[0].astype(F32) + sgm_ref[0].astype(F32) * m_out
    mix = jnp.dot(y.astype(BF16), wo_ref[...], preferred_element_type=F32)
    x1 = x_ref[0] + g1_ref[0] * mix

    a2 = ((x1 * _rms_scale(x1, d)) * nf_ref[...] * (1.0 + sc2_ref[0]) + sh2_ref[0]).astype(BF16)
    acc = jnp.zeros_like(x1)
    for c0 in range(0, f, FFN_CHUNK):
        gate = jnp.dot(a2, wi_ref[:, c0:c0 + FFN_CHUNK], preferred_element_type=F32)
        up = jnp.dot(a2, wi_ref[:, f + c0:f + c0 + FFN_CHUNK], preferred_element_type=F32)
        act = (_silu(gate) * up).astype(BF16)
        acc = acc + jnp.dot(act, wo2_ref[c0:c0 + FFN_CHUNK, :], preferred_element_type=F32)
    x2 = x1 + g2_ref[0] * acc
    o_ref[0] = (x2 * _rms_scale(x2, d)) * nfin_ref[...]


def _tail(x, hf, hb, ypool, proj, so_col, sgm_col, mod, mh_gain, w_out, norm_ffn, w_ffn_in, w_ffn_out,
          norm_final):
    b, n, d = x.shape
    t = min(TOKEN_TILE, n)
    assert w_ffn_out.shape[0] % FFN_CHUNK == 0

    def tile(col=0):
        return pl.BlockSpec((1, t, d), lambda bi, i: (bi, i, col))

    def mod_spec(col):
        return pl.BlockSpec((1, 1, d), lambda bi, i: (bi, 0, col))

    return pl.pallas_call(
        _tail_kernel,
        grid=(b, n // t),
        in_specs=[tile(), tile(), tile(), tile(), tile(so_col), tile(sgm_col),
                  mod_spec(2), mod_spec(3), mod_spec(4), mod_spec(5),
                  _resident(mh_gain.shape), _resident(w_out.shape), _resident(norm_ffn.shape),
                  _resident(w_ffn_in.shape), _resident(w_ffn_out.shape), _resident(norm_final.shape)],
        out_specs=tile(),
        out_shape=jax.ShapeDtypeStruct((b, n, d), F32),
        compiler_params=_params(("arbitrary", "arbitrary")),
        name="tail",
    )(x, hf, hb, ypool, proj, proj, mod, mod, mod, mod, mh_gain, w_out, norm_ffn, w_ffn_in, w_ffn_out,
      norm_final)


def kernel(x, c, ctx, c_ctx, norm_mix, norm_ffn, norm_final, w_ada, b_ada, w_in, b_gates, conv_w, conv_b,
           w_pool, pool_scale, mh_gain, w_out, w_ffn_in, w_ffn_out):
    assert w_ada.shape[0] == 1, "single-layer stack only"
    b, n, d = x.shape
    assert d % (128 * M_HEADS) == 0 and w_in.shape[2] == 7 * d + N_GATE_COLS
    dh = d // M_HEADS

    pad_rows = -(b + 1) % 8
    c_rows = jnp.concatenate([c, c_ctx[None], jnp.zeros((pad_rows, d), F32)], axis=0)
    mod_all = _ada(c_rows, w_ada[0], b_ada)
    mod = mod_all[:b, None, :]
    mod_ctx = mod_all[b:b + 1, None, :]

    w_main = w_in[0, :, :7 * d].astype(BF16)
    w_gate = jnp.pad(w_in[0, :, 7 * d:], ((0, 0), (0, GATE_LANES - N_GATE_COLS))).astype(BF16)
    b_gate = jnp.pad(b_gates, ((0, 0), (0, GATE_LANES - N_GATE_COLS)))
    q_scale = dh ** -0.5

    lat_kinds = ("plain", "q", "k", "plain", "sig", "sig", "sig")
    proj, kt, gates = _in_proj(x, mod, (0, 1), norm_mix, w_main, w_gate, conv_w[0], conv_b, b_gate,
                               lat_kinds, q_scale)
    proj_c, kt_c, gates_c = _in_proj(ctx, mod_ctx, (0, 1), norm_mix, w_main[:, d:4 * d], w_gate, conv_w[0],
                                     conv_b, b_gate, ("q", "k", "plain"), q_scale)

    l_lat = min(SCAN_CHUNK, n)
    l_ctx = min(SCAN_CHUNK, ctx.shape[1])
    h_dirs = []
    for backward in (False, True):
        *info_c, m_c = _gateprep(gates_c, l_ctx, backward, None)
        *info, _ = _gateprep(gates, l_lat, backward, m_c)
        state = _mlstm(proj_c, kt_c, d, 0, 1, info_c, None, backward, False, True)
        (h_dir,) = _mlstm(proj, kt, d, 1, 2, info, state, backward, True, False)
        h_dirs.append(h_dir)

    ypool = _pool(proj, d, 0, 4, _band_matrices(), w_pool[0].astype(BF16), pool_scale)
    return _tail(x, h_dirs[0], h_dirs[1], ypool, proj, 3, 5, mod, mh_gain, w_out[0].astype(BF16), norm_ffn,
                 w_ffn_in[0].astype(BF16), w_ffn_out[0].astype(BF16), norm_final[None])
```

```python
import functools
import math

import numpy as np
import jax
import jax.numpy as jnp
from jax import lax
from jax.experimental import pallas as pl
from jax.experimental.pallas import tpu as pltpu

F32 = jnp.float32
BF16 = jnp.bfloat16

EPS = 1e-6
LOG2E = math.log2(math.e)
GRID_W = 64
POOL_WINDOWS = (2, 4, 8, 16)
M_HEADS = 4
CONV_W = 3
N_GATE_COLS = 4 * M_HEADS
GATE_LANES = 128
SCAN_CHUNK = 256
TOKEN_TILE = 512
HALO_ROWS = 16
VMEM_LIMIT = 60 * 1024 * 1024


def _resident(shape):
    nd = len(shape)
    return pl.BlockSpec(shape, lambda *_: (0,) * nd, pipeline_mode=pl.Buffered(1))


def _params(sem):
    return pltpu.CompilerParams(dimension_semantics=sem, vmem_limit_bytes=VMEM_LIMIT)


def _sigmoid(x):
    return 1.0 / (1.0 + jnp.exp(-x))


def _silu(x):
    return x * _sigmoid(x)


def _rms_scale(xf, d):
    return lax.rsqrt(jnp.sum(xf * xf, axis=-1, keepdims=True) * (1.0 / d) + EPS)


def _ada_kernel(c_ref, w_ref, b_ref, o_ref):
    s = _silu(c_ref[...])
    o_ref[...] = jnp.dot(s.astype(BF16), w_ref[...].astype(BF16), preferred_element_type=F32) + b_ref[...]


def _ada(c_rows, w_ada, b_ada):
    r, d = c_rows.shape
    n = w_ada.shape[1]
    return pl.pallas_call(
        _ada_kernel,
        grid=(n // d,),
        in_specs=[pl.BlockSpec((r, d), lambda j: (0, 0)),
                  pl.BlockSpec((d, d), lambda j: (0, j)),
                  pl.BlockSpec((1, d), lambda j: (0, j))],
        out_specs=pl.BlockSpec((r, d), lambda j: (0, j)),
        out_shape=jax.ShapeDtypeStruct((r, n), F32),
        compiler_params=_params(("arbitrary",)),
        name="ada",
    )(c_rows, w_ada, b_ada)


def _cast_kernel(w_ref, o_ref):
    o_ref[...] = w_ref[...].astype(BF16)


def _cast_bf16(w, ncols=None):
    k, n = w.shape
    ncols = n if ncols is None else ncols
    block = max(bw for bw in range(128, min(ncols, 1024) + 1, 128) if ncols % bw == 0)
    return pl.pallas_call(
        _cast_kernel,
        grid=(ncols // block,),
        in_specs=[pl.BlockSpec((k, block), lambda j: (0, j))],
        out_specs=pl.BlockSpec((k, block), lambda j: (0, j)),
        out_shape=jax.ShapeDtypeStruct((k, ncols), BF16),
        compiler_params=_params(("arbitrary",)),
        name="wcast",
    )(w)


def _in_proj_kernel(kinds, q_scale, xp_ref, xc_ref, xn_ref, sh_ref, sc_ref, gain_ref, w_ref, wg_ref,
                    cw_ref, cb_ref, bg_ref, proj_ref, kt_ref, gates_ref, a_ref, qk_ref):
    i = pl.program_id(1)
    nt = pl.num_programs(1)
    t = xc_ref.shape[1]
    d = xc_ref.shape[2]
    gain = gain_ref[...]
    mult = 1.0 + sc_ref[0]
    shift = sh_ref[0]

    def norm_mod(xf):
        return (xf * _rms_scale(xf, d)) * gain * mult + shift

    prev = jnp.where(i > 0, norm_mod(xp_ref[0]), 0.0)
    nxt = jnp.where(i < nt - 1, norm_mod(xn_ref[0]), 0.0)
    zeros8 = jnp.zeros((HALO_ROWS - 8, d), F32)
    a_ref[0:HALO_ROWS, :] = jnp.concatenate([zeros8, prev], axis=0).astype(BF16)
    a_ref[HALO_ROWS:HALO_ROWS + t, :] = norm_mod(xc_ref[0]).astype(BF16)
    a_ref[HALO_ROWS + t:2 * HALO_ROWS + t, :] = jnp.concatenate([nxt, zeros8], axis=0).astype(BF16)

    out_col = 0
    for g, kind in enumerate(kinds):
        cols = slice(g * d, (g + 1) * d)
        if kind == "skip":
            continue
        if kind in ("q", "k"):
            qk_ref[...] = jnp.dot(a_ref[...], w_ref[:, cols], preferred_element_type=F32)
            c0 = 0 if kind == "q" else d
            cw = cw_ref[:, c0:c0 + d]
            y = (cb_ref[:, c0:c0 + d]
                 + qk_ref[HALO_ROWS - 1:HALO_ROWS - 1 + t, :] * cw[0:1]
                 + qk_ref[HALO_ROWS:HALO_ROWS + t, :] * cw[1:2]
                 + qk_ref[HALO_ROWS + 1:HALO_ROWS + 1 + t, :] * cw[2:3])
            y = _silu(y)
            if kind == "q":
                y = y * q_scale
        else:
            y = jnp.dot(a_ref[HALO_ROWS:HALO_ROWS + t, :], w_ref[:, cols], preferred_element_type=F32)
            if kind == "sig":
                y = _sigmoid(y)
        if kind == "k":
            kt_ref[0] = y.T.astype(BF16)
        else:
            proj_ref[0, :, out_col * d:(out_col + 1) * d] = y.astype(BF16)
            out_col += 1

    gates = jnp.dot(a_ref[HALO_ROWS:HALO_ROWS + t, :], wg_ref[...], preferred_element_type=F32) + bg_ref[...]
    gates_ref[0] = gates.T[0:N_GATE_COLS, :]


def _in_proj(x, mod, mod_cols, gain, w, wg, conv_w, conv_b, bg, kinds, q_scale):
    b, n, d = x.shape
    t = min(TOKEN_TILE, n)
    nt = n // t
    per_batch = mod.shape[0] > 1
    sh_col, sc_col = mod_cols

    def mod_map(col):
        return lambda bi, i: (bi if per_batch else 0, 0, col)

    t8 = t // 8
    last8 = n // 8 - 1
    in_specs = [
        pl.BlockSpec((1, 8, d), lambda bi, i: (bi, jnp.maximum(i * t8 - 1, 0), 0)),
        pl.BlockSpec((1, t, d), lambda bi, i: (bi, i, 0)),
        pl.BlockSpec((1, 8, d), lambda bi, i: (bi, jnp.minimum((i + 1) * t8, last8), 0)),
        pl.BlockSpec((1, 1, d), mod_map(sh_col)),
        pl.BlockSpec((1, 1, d), mod_map(sc_col)),
        _resident(gain.shape), _resident((d, len(kinds) * d)), _resident(wg.shape),
        _resident(conv_w.shape), _resident(conv_b.shape), _resident(bg.shape),
    ]
    assert kinds.count("k") == 1
    ncol = (len(kinds) - 1 - kinds.count("skip")) * d
    return pl.pallas_call(
        functools.partial(_in_proj_kernel, kinds, q_scale),
        grid=(b, nt),
        in_specs=in_specs,
        out_specs=[pl.BlockSpec((1, t, ncol), lambda bi, i: (bi, i, 0)),
                   pl.BlockSpec((1, d, t), lambda bi, i: (bi, 0, i)),
                   pl.BlockSpec((1, N_GATE_COLS, t), lambda bi, i: (bi, 0, i))],
        out_shape=[jax.ShapeDtypeStruct((b, n, ncol), BF16),
                   jax.ShapeDtypeStruct((b, d, n), BF16),
                   jax.ShapeDtypeStruct((b, N_GATE_COLS, n), F32)],
        scratch_shapes=[pltpu.VMEM((t + 2 * HALO_ROWS, d), BF16),
                        pltpu.VMEM((t + 2 * HALO_ROWS, d), F32)],
        compiler_params=_params(("arbitrary", "arbitrary")),
        name="in_proj",
    )(x, x, x, mod, mod, gain, w, wg, conv_w, conv_b, bg)


GROW_ROWS = 4 * N_GATE_COLS


def _gate_direction(x, m_init, l, backward):
    n = x.shape[1]
    nc = n // l
    pos = lax.broadcasted_iota(jnp.int32, x.shape, 1) & (l - 1)
    neg = jnp.float32(-jnp.inf)

    def shifted(v, s, fill):
        if backward:
            return jnp.where(pos < l - s, pltpu.roll(v, n - s, axis=1), fill)
        return jnp.where(pos >= s, pltpu.roll(v, s, axis=1), fill)

    csum = jnp.minimum(x, 0.0) - jnp.log1p(jnp.exp(-jnp.abs(x)))
    s = 1
    while s < l:
        csum = csum + shifted(csum, s, 0.0)
        s *= 2
    g = pltpu.roll(x, 4, axis=0) - csum
    gmax = g
    s = 1
    while s < l:
        gmax = jnp.maximum(gmax, shifted(gmax, s, neg))
        s *= 2

    m0_parts, ml_parts = [None] * nc, [None] * nc
    m0 = m_init
    for step in range(nc):
        ci = nc - 1 - step if backward else step
        last = ci * l if backward else ci * l + l - 1
        m_last = jnp.maximum(m0, gmax[:, last:last + 1])
        m0_parts[ci] = jnp.broadcast_to(m0, (8, l))
        ml_parts[ci] = jnp.broadcast_to(m_last, (8, l))
        m0 = csum[:, last:last + 1] + m_last
    m0_arr = jnp.concatenate(m0_parts, axis=1)
    ml_arr = jnp.concatenate(ml_parts, axis=1)
    big_m = jnp.maximum(m0_arr, gmax)
    f_rows = lax.broadcasted_iota(jnp.int32, x.shape, 0) >= 4
    rows = (g * LOG2E, jnp.exp(g - ml_arr), m0_arr * LOG2E, jnp.exp(m0_arr - ml_arr))
    cols = (big_m * LOG2E, jnp.exp(-(csum + big_m)))
    rows = tuple(jnp.where(f_rows, v, 0.0) for v in rows)
    cols = tuple(jnp.where(f_rows, v, 0.0) for v in cols)
    return rows, cols, m0


def _gateprep_kernel(l, has_init, *refs):
    refs = list(refs)
    gates_ref = refs.pop(0)
    m_init_ref = refs.pop(0) if has_init else None
    gcol_ref, grow_ref, m_out_ref = refs
    n = gates_ref.shape[2]
    col_blocks = []
    for d in range(2):
        rows8 = slice(8 * d, 8 * d + 8)
        m_init = m_init_ref[0, rows8, 0:1] if has_init else jnp.zeros((8, 1), F32)
        rows, cols, m_fin = _gate_direction(gates_ref[0, rows8, :], m_init, l, backward=d == 1)
        for k, v in enumerate(rows):
            grow_ref[0, k * N_GATE_COLS + 8 * d:k * N_GATE_COLS + 8 * d + 8, :] = v
        col_blocks.append(cols)
        m_out_ref[0, rows8, :] = jnp.broadcast_to(m_fin, (8, GATE_LANES))
    col_src = jnp.concatenate([col_blocks[0][0], col_blocks[1][0], col_blocks[0][1], col_blocks[1][1],
                               jnp.zeros((GATE_LANES - 2 * N_GATE_COLS, n), F32)], axis=0)
    gcol_ref[0] = col_src.T


def _gateprep(gates, l, m_init):
    b, _, n = gates.shape
    assert l & (l - 1) == 0 and n % l == 0
    in_specs = [pl.BlockSpec((1, N_GATE_COLS, n), lambda bi: (bi, 0, 0))]
    args = [gates]
    if m_init is not None:
        in_specs.append(pl.BlockSpec((1, N_GATE_COLS, GATE_LANES), lambda bi: (bi, 0, 0)))
        args.append(m_init)
    return pl.pallas_call(
        functools.partial(_gateprep_kernel, l, m_init is not None),
        grid=(b,),
        in_specs=in_specs,
        out_specs=[pl.BlockSpec((1, n, GATE_LANES), lambda bi: (bi, 0, 0)),
                   pl.BlockSpec((1, GROW_ROWS, n), lambda bi: (bi, 0, 0)),
                   pl.BlockSpec((1, N_GATE_COLS, GATE_LANES), lambda bi: (bi, 0, 0))],
        out_shape=[jax.ShapeDtypeStruct((b, n, GATE_LANES), F32),
                   jax.ShapeDtypeStruct((b, GROW_ROWS, n), F32),
                   jax.ShapeDtypeStruct((b, N_GATE_COLS, GATE_LANES), F32)],
        compiler_params=_params(("arbitrary",)),
        name="gateprep",
    )(*args)


def _mlstm_kernel(backward, has_init, emit_h, emit_state, *refs):
    refs = list(refs)
    q_ref, kt_ref, v_ref, gcol_ref, grow_ref = refs[:5]
    pos = 5
    if has_init:
        c0_ref, n0_ref = refs[pos:pos + 2]
        pos += 2
    if emit_h:
        h_ref = refs[pos]
        pos += 1
    if emit_state:
        co_ref, no_ref = refs[pos:pos + 2]
        pos += 2
    c_s, cb_s, n_s, nb_s = refs[pos:pos + 4]

    j = pl.program_id(1)
    nj = pl.num_programs(1)
    l = q_ref.shape[1]
    dh = q_ref.shape[2] // M_HEADS

    @pl.when(j == 0)
    def _init():
        if has_init:
            c_s[...] = c0_ref[0]
            cb_s[...] = c0_ref[0].astype(BF16)
            n_s[...] = n0_ref[0]
            nb_s[...] = n0_ref[0].astype(BF16)
        else:
            c_s[...] = jnp.zeros_like(c_s)
            cb_s[...] = jnp.zeros_like(cb_s)
            n_s[...] = jnp.zeros_like(n_s)
            nb_s[...] = jnp.zeros_like(nb_s)

    r = lax.broadcasted_iota(jnp.int32, (l, l), 0)
    c = lax.broadcasted_iota(jnp.int32, (l, l), 1)
    visible = (c >= r) if backward else (c <= r)
    neg = jnp.float32(-jnp.inf)
    ones_rhs = jnp.ones((l, GATE_LANES), BF16)
    gc = gcol_ref[0]

    for h in range(M_HEADS):
        lane = (12 if backward else 4) + h
        cols = slice(h * dh, (h + 1) * dh)
        q = q_ref[0, :, cols]
        kt = kt_ref[0, cols, :]
        v = v_ref[0, :, cols]
        g_row = grow_ref[0, lane:lane + 1, :]
        ws_row = grow_ref[0, lane + N_GATE_COLS:lane + N_GATE_COLS + 1, :]
        m0 = grow_ref[0, lane + 2 * N_GATE_COLS:lane + 2 * N_GATE_COLS + 1, 0:1]
        s0 = grow_ref[0, lane + 3 * N_GATE_COLS:lane + 3 * N_GATE_COLS + 1, 0:1]

        if emit_h:
            m_col = gc[:, lane:lane + 1]
            floor = jnp.broadcast_to(gc[:, lane + N_GATE_COLS:lane + N_GATE_COLS + 1], (l, GATE_LANES))
            qk = jnp.dot(q, kt, preferred_element_type=F32)
            s_mat = (qk * jnp.exp2(jnp.where(visible, g_row - m_col, neg))).astype(BF16)
            sa = jnp.exp2(m0 - jnp.broadcast_to(m_col, (l, GATE_LANES)))
            num = (jnp.dot(s_mat, v, preferred_element_type=F32)
                   + jnp.concatenate([sa] * (dh // GATE_LANES), axis=1)
                   * jnp.dot(q, cb_s[h], preferred_element_type=F32))
            den = (jnp.dot(s_mat, ones_rhs, preferred_element_type=F32)
                   + sa * jnp.dot(q, nb_s[h], preferred_element_type=F32))
            rcp = 1.0 / jnp.maximum(jnp.abs(den), floor)
            h_ref[0, :, cols] = num * jnp.concatenate([rcp] * (dh // GATE_LANES), axis=1)

        ktw = (kt.astype(F32) * ws_row).astype(BF16)
        c_new = s0 * c_s[h] + jnp.dot(ktw, v, preferred_element_type=F32)
        n_new = s0 * n_s[h] + jnp.dot(ktw, ones_rhs, preferred_element_type=F32)
        c_s[h] = c_new
        cb_s[h] = c_new.astype(BF16)
        n_s[h] = n_new
        nb_s[h] = n_new.astype(BF16)

    if emit_state:
        @pl.when(j == nj - 1)
        def _emit():
            co_ref[0] = c_s[...]
            no_ref[0] = n_s[...]


def _mlstm(proj, kt, dm, q_col, v_col, gcol, grow, l, init, backward, emit_h, emit_state):
    b, n, _ = proj.shape
    nc = n // l
    dh = dm // M_HEADS

    def chunk(j):
        return nc - 1 - j if backward else j

    in_specs = [pl.BlockSpec((1, l, dm), lambda bi, j: (bi, chunk(j), q_col)),
                pl.BlockSpec((1, dm, l), lambda bi, j: (bi, 0, chunk(j))),
                pl.BlockSpec((1, l, dm), lambda bi, j: (bi, chunk(j), v_col)),
                pl.BlockSpec((1, l, GATE_LANES), lambda bi, j: (bi, chunk(j), 0)),
                pl.BlockSpec((1, GROW_ROWS, l), lambda bi, j: (bi, 0, chunk(j)))]
    args = [proj, kt, proj, gcol, grow]
    state_specs = [pl.BlockSpec((1, M_HEADS, dh, dh), lambda bi, j: (bi, 0, 0, 0)),
                   pl.BlockSpec((1, M_HEADS, dh, GATE_LANES), lambda bi, j: (bi, 0, 0, 0))]
    state_shapes = [jax.ShapeDtypeStruct((b, M_HEADS, dh, dh), F32),
                    jax.ShapeDtypeStruct((b, M_HEADS, dh, GATE_LANES), F32)]
    if init is not None:
        in_specs += state_specs
        args += list(init)
    out_specs, out_shape = [], []
    if emit_h:
        out_specs.append(pl.BlockSpec((1, l, dm), lambda bi, j: (bi, chunk(j), 0)))
        out_shape.append(jax.ShapeDtypeStruct((b, n, dm), F32))
    if emit_state:
        out_specs += state_specs
        out_shape += state_shapes
    return pl.pallas_call(
        functools.partial(_mlstm_kernel, backward, init is not None, emit_h, emit_state),
        grid=(b, nc),
        in_specs=in_specs,
        out_specs=out_specs,
        out_shape=out_shape,
        scratch_shapes=[pltpu.VMEM((M_HEADS, dh, dh), F32), pltpu.VMEM((M_HEADS, dh, dh), BF16),
                        pltpu.VMEM((M_HEADS, dh, GATE_LANES), F32),
                        pltpu.VMEM((M_HEADS, dh, GATE_LANES), BF16)],
        compiler_params=_params(("arbitrary", "arbitrary")),
        name="mlstm_bwd" if backward else "mlstm_fwd",
    )(*args)


BAND = 256


def _band_matrices():
    t = np.arange(BAND)
    same_row = (t[:, None] // GRID_W) == (t[None, :] // GRID_W)
    ct, cs = t[:, None] % GRID_W, t[None, :] % GRID_W
    mats = [same_row & (cs >= ct - w // 2) & (cs <= ct + w // 2 - 1) for w in POOL_WINDOWS]
    return jnp.asarray(np.stack(mats).astype(np.float32), dtype=BF16)


def _pool_kernel(up_ref, uc_ref, un_ref, sg_ref, band_ref, wp_ref, ps_ref, y_ref, cs_ref):
    i = pl.program_id(1)
    nt = pl.num_programs(1)
    t = uc_ref.shape[1]
    gdim = wp_ref.shape[1]
    rt = t // GRID_W
    shift = GRID_W.bit_length() - 1
    tok = lax.broadcasted_iota(jnp.int32, (t, 1), 0)
    row = i * rt + lax.shift_right_logical(tok, shift)
    col = tok & (GRID_W - 1)
    prev_ok = i > 0
    next_ok = i < nt - 1

    for g, w in enumerate(POOL_WINDOWS):
        hw = w // 2
        cols = slice(g * gdim, (g + 1) * gdim)
        band = band_ref[g]
        lo = (rt - hw) * GRID_W // BAND * BAND
        hi = -(-((2 * rt + hw - 1) * GRID_W) // BAND) * BAND
        for b0 in range(lo, hi, BAND):
            seg, off = divmod(b0, t)
            if seg == 0:
                blk = jnp.where(prev_ok, up_ref[0, off:off + BAND, cols], 0)
            elif seg == 1:
                blk = uc_ref[0, off:off + BAND, cols]
            else:
                blk = jnp.where(next_ok, un_ref[0, off:off + BAND, cols], 0)
            cs_ref[b0:b0 + BAND, :] = jnp.dot(band, blk, preferred_element_type=F32)
        acc = cs_ref[(rt - hw) * GRID_W:(rt - hw) * GRID_W + t, :]
        for jr in range(-hw + 1, hw):
            acc = acc + cs_ref[(rt + jr) * GRID_W:(rt + jr) * GRID_W + t, :]
        cnt_r = jnp.minimum(row + hw, nt * rt) - jnp.maximum(row - hw, 0)
        cnt_c = jnp.minimum(col + hw, GRID_W) - jnp.maximum(col - hw, 0)
        inv = 1.0 / (cnt_r * cnt_c).astype(F32)
        delta = acc * inv - uc_ref[0, :, cols].astype(F32)
        po = jnp.dot(delta.astype(BF16), wp_ref[g], preferred_element_type=F32)
        y_ref[0, :, cols] = (sg_ref[0, :, cols].astype(F32) * (po * ps_ref[:, cols])).astype(BF16)


def _pool(proj, dm, u_col, sg_col, bands, w_pool, pool_scale):
    b, n, _ = proj.shape
    t = TOKEN_TILE
    assert n % t == 0 and t % BAND == 0 and BAND % GRID_W == 0 and GRID_W & (GRID_W - 1) == 0
    assert t // GRID_W >= max(POOL_WINDOWS) // 2
    nt = n // t
    gdim = dm // len(POOL_WINDOWS)
    return pl.pallas_call(
        _pool_kernel,
        grid=(b, nt),
        in_specs=[pl.BlockSpec((1, t, dm), lambda bi, i: (bi, jnp.maximum(i - 1, 0), u_col)),
                  pl.BlockSpec((1, t, dm), lambda bi, i: (bi, i, u_col)),
                  pl.BlockSpec((1, t, dm), lambda bi, i: (bi, jnp.minimum(i + 1, nt - 1), u_col)),
                  pl.BlockSpec((1, t, dm), lambda bi, i: (bi, i, sg_col)),
                  _resident(bands.shape), _resident(w_pool.shape), _resident(pool_scale.shape)],
        out_specs=pl.BlockSpec((1, t, dm), lambda bi, i: (bi, i, 0)),
        out_shape=jax.ShapeDtypeStruct((b, n, dm), BF16),
        scratch_shapes=[pltpu.VMEM((3 * t, gdim), F32)],
        compiler_params=_params(("arbitrary", "arbitrary")),
        name="pool",
    )(proj, proj, proj, proj, bands, w_pool, pool_scale)


FFN_CHUNK = 256


def _tail_kernel(x_ref, hf_ref, hb_ref, yp_ref, so_ref, sgm_ref, g1_ref, sh2_ref, sc2_ref, g2_ref,
                 mhg_ref, wo_ref, nf_ref, wi_ref, wo2_ref, nfin_ref, o_ref):
    d = x_ref.shape[2]
    dh = d // M_HEADS
    f = wo2_ref.shape[0]

    hs = hf_ref[0] + hb_ref[0]
    parts = []
    for h in range(M_HEADS):
        hh = hs[:, h * dh:(h + 1) * dh]
        parts.append(hh * _rms_scale(hh, dh))
    m_out = jnp.concatenate(parts, axis=-1) * mhg_ref[...] * so_ref[0].astype(F32)
    y = yp_ref[0].astype(F32) + sgm_ref[0].astype(F32) * m_out
    mix = jnp.dot(y.astype(BF16), wo_ref[...], preferred_element_type=F32)
    x1 = x_ref[0] + g1_ref[0] * mix

    a2 = ((x1 * _rms_scale(x1, d)) * nf_ref[...] * (1.0 + sc2_ref[0]) + sh2_ref[0]).astype(BF16)
    acc = jnp.zeros_like(x1)
    for c0 in range(0, f, FFN_CHUNK):
        gate = jnp.dot(a2, wi_ref[:, c0:c0 + FFN_CHUNK], preferred_element_type=F32)
        up = jnp.dot(a2, wi_ref[:, f + c0:f + c0 + FFN_CHUNK], preferred_element_type=F32)
        act = (_silu(gate) * up).astype(BF16)
        acc = acc + jnp.dot(act, wo2_ref[c0:c0 + FFN_CHUNK, :], preferred_element_type=F32)
    x2 = x1 + g2_ref[0] * acc
    o_ref[0] = (x2 * _rms_scale(x2, d)) * nfin_ref[...]


def _tail(x, hf, hb, ypool, proj, so_col, sgm_col, mod, mh_gain, w_out, norm_ffn, w_ffn_in, w_ffn_out,
          norm_final):
    b, n, d = x.shape
    t = min(TOKEN_TILE, n)
    assert w_ffn_out.shape[0] % FFN_CHUNK == 0

    def tile(col=0):
        return pl.BlockSpec((1, t, d), lambda bi, i: (bi, i, col))

    def mod_spec(col):
        return pl.BlockSpec((1, 1, d), lambda bi, i: (bi, 0, col))

    return pl.pallas_call(
        _tail_kernel,
        grid=(b, n // t),
        in_specs=[tile(), tile(), tile(), tile(), tile(so_col), tile(sgm_col),
                  mod_spec(2), mod_spec(3), mod_spec(4), mod_spec(5),
                  _resident(mh_gain.shape), _resident(w_out.shape), _resident(norm_ffn.shape),
                  _resident(w_ffn_in.shape), _resident(w_ffn_out.shape), _resident(norm_final.shape)],
        out_specs=tile(),
        out_shape=jax.ShapeDtypeStruct((b, n, d), F32),
        compiler_params=_params(("arbitrary", "arbitrary")),
        name="tail",
    )(x, hf, hb, ypool, proj, proj, mod, mod, mod, mod, mh_gain, w_out, norm_ffn, w_ffn_in, w_ffn_out,
      norm_final)


def kernel(x, c, ctx, c_ctx, norm_mix, norm_ffn, norm_final, w_ada, b_ada, w_in, b_gates, conv_w, conv_b,
           w_pool, pool_scale, mh_gain, w_out, w_ffn_in, w_ffn_out):
    assert w_ada.shape[0] == 1, "single-layer stack only"
    b, n, d = x.shape
    assert d % (128 * M_HEADS) == 0 and w_in.shape[2] == 7 * d + N_GATE_COLS
    dh = d // M_HEADS

    pad_rows = -(b + 1) % 8
    c_rows = jnp.concatenate([c, c_ctx[None], jnp.zeros((pad_rows, d), F32)], axis=0)
    mod_all = _ada(c_rows, w_ada[0], b_ada)
    mod = mod_all[:b, None, :]
    mod_ctx = mod_all[b:b + 1, None, :]

    w_main = _cast_bf16(w_in[0], 7 * d)
    w_gate = jnp.pad(w_in[0, :, 7 * d:], ((0, 0), (0, GATE_LANES - N_GATE_COLS))).astype(BF16)
    b_gate = jnp.pad(b_gates, ((0, 0), (0, GATE_LANES - N_GATE_COLS)))
    q_scale = dh ** -0.5

    lat_kinds = ("plain", "q", "k", "plain", "sig", "sig", "sig")
    proj, kt, gates = _in_proj(x, mod, (0, 1), norm_mix, w_main, w_gate, conv_w[0], conv_b, b_gate,
                               lat_kinds, q_scale)
    proj_c, kt_c, gates_c = _in_proj(ctx, mod_ctx, (0, 1), norm_mix, w_main, w_gate, conv_w[0],
                                     conv_b, b_gate, ("skip", "q", "k", "plain"), q_scale)

    l_lat = min(SCAN_CHUNK, n)
    l_ctx = min(SCAN_CHUNK, ctx.shape[1])
    gcol_c, grow_c, m_c = _gateprep(gates_c, l_ctx, None)
    gcol, grow, _ = _gateprep(gates, l_lat, m_c)
    h_dirs = []
    for backward in (False, True):
        state = _mlstm(proj_c, kt_c, d, 0, 1, gcol_c, grow_c, l_ctx, None, backward, False, True)
        (h_dir,) = _mlstm(proj, kt, d, 1, 2, gcol, grow, l_lat, state, backward, True, False)
        h_dirs.append(h_dir)

    groups, gdim = w_pool.shape[1], w_pool.shape[2]
    w_pool_b = _cast_bf16(w_pool[0].reshape(groups * gdim, gdim)).reshape(groups, gdim, gdim)
    ypool = _pool(proj, d, 0, 4, _band_matrices(), w_pool_b, pool_scale)
    return _tail(x, h_dirs[0], h_dirs[1], ypool, proj, 3, 5, mod, mh_gain, _cast_bf16(w_out[0]), norm_ffn,
                 _cast_bf16(w_ffn_in[0]), _cast_bf16(w_ffn_out[0]), norm_final[None])
```

```python
import functools
import math

import numpy as np
import jax
import jax.numpy as jnp
from jax import lax
from jax.experimental import pallas as pl
from jax.experimental.pallas import tpu as pltpu

F32 = jnp.float32
BF16 = jnp.bfloat16

EPS = 1e-6
LOG2E = math.log2(math.e)
GRID_W = 64
POOL_WINDOWS = (2, 4, 8, 16)
M_HEADS = 4
CONV_W = 3
N_GATE_COLS = 4 * M_HEADS
GATE_LANES = 128
SCAN_CHUNK = 256
TOKEN_TILE = 512
HALO_ROWS = 16
VMEM_LIMIT = 60 * 1024 * 1024


def _resident(shape):
    nd = len(shape)
    return pl.BlockSpec(shape, lambda *_: (0,) * nd, pipeline_mode=pl.Buffered(1))


def _params(sem):
    return pltpu.CompilerParams(dimension_semantics=sem, vmem_limit_bytes=VMEM_LIMIT)


def _sigmoid(x):
    return 1.0 / (1.0 + jnp.exp(-x))


def _silu(x):
    return x * _sigmoid(x)


def _rms_scale(xf, d):
    return lax.rsqrt(jnp.sum(xf * xf, axis=-1, keepdims=True) * (1.0 / d) + EPS)


def _ada_kernel(c_ref, w_ref, b_ref, o_ref):
    s = _silu(c_ref[...])
    o_ref[...] = jnp.dot(s.astype(BF16), w_ref[...].astype(BF16), preferred_element_type=F32) + b_ref[...]


def _ada(c_rows, w_ada, b_ada):
    r, d = c_rows.shape
    n = w_ada.shape[1]
    return pl.pallas_call(
        _ada_kernel,
        grid=(n // d,),
        in_specs=[pl.BlockSpec((r, d), lambda j: (0, 0)),
                  pl.BlockSpec((d, d), lambda j: (0, j)),
                  pl.BlockSpec((1, d), lambda j: (0, j))],
        out_specs=pl.BlockSpec((r, d), lambda j: (0, j)),
        out_shape=jax.ShapeDtypeStruct((r, n), F32),
        compiler_params=_params(("arbitrary",)),
        name="ada",
    )(c_rows, w_ada, b_ada)


def _cast_kernel(w_ref, o_ref):
    o_ref[...] = w_ref[...].astype(BF16)


def _cast_bf16(w, ncols=None):
    k, n = w.shape
    ncols = n if ncols is None else ncols
    block = max(bw for bw in range(128, min(ncols, 1024) + 1, 128) if ncols % bw == 0)
    return pl.pallas_call(
        _cast_kernel,
        grid=(ncols // block,),
        in_specs=[pl.BlockSpec((k, block), lambda j: (0, j))],
        out_specs=pl.BlockSpec((k, block), lambda j: (0, j)),
        out_shape=jax.ShapeDtypeStruct((k, ncols), BF16),
        compiler_params=_params(("arbitrary",)),
        name="wcast",
    )(w)


def _in_proj_kernel(kinds, q_scale, xp_ref, xc_ref, xn_ref, sh_ref, sc_ref, gain_ref, w_ref, wg_ref,
                    cw_ref, cb_ref, bg_ref, *out_and_scratch):
    n_tr = sum(dest == "tr" for _, dest in kinds)
    proj_ref = out_and_scratch[0]
    tr_refs = out_and_scratch[1:1 + n_tr]
    gates_ref, a_ref, qk_ref = out_and_scratch[1 + n_tr:]
    i = pl.program_id(1)
    nt = pl.num_programs(1)
    t = xc_ref.shape[1]
    d = xc_ref.shape[2]
    gain = gain_ref[...]
    mult = 1.0 + sc_ref[0]
    shift = sh_ref[0]

    def norm_mod(xf):
        return (xf * _rms_scale(xf, d)) * gain * mult + shift

    prev = jnp.where(i > 0, norm_mod(xp_ref[0]), 0.0)
    nxt = jnp.where(i < nt - 1, norm_mod(xn_ref[0]), 0.0)
    zeros8 = jnp.zeros((HALO_ROWS - 8, d), F32)
    a_ref[0:HALO_ROWS, :] = jnp.concatenate([zeros8, prev], axis=0).astype(BF16)
    a_ref[HALO_ROWS:HALO_ROWS + t, :] = norm_mod(xc_ref[0]).astype(BF16)
    a_ref[HALO_ROWS + t:2 * HALO_ROWS + t, :] = jnp.concatenate([nxt, zeros8], axis=0).astype(BF16)

    out_col = 0
    out_tr = 0
    for g, (kind, dest) in enumerate(kinds):
        cols = slice(g * d, (g + 1) * d)
        if kind == "skip":
            continue
        if kind in ("q", "k"):
            qk_ref[...] = jnp.dot(a_ref[...], w_ref[:, cols], preferred_element_type=F32)
            c0 = 0 if kind == "q" else d
            cw = cw_ref[:, c0:c0 + d]
            y = (cb_ref[:, c0:c0 + d]
                 + qk_ref[HALO_ROWS - 1:HALO_ROWS - 1 + t, :] * cw[0:1]
                 + qk_ref[HALO_ROWS:HALO_ROWS + t, :] * cw[1:2]
                 + qk_ref[HALO_ROWS + 1:HALO_ROWS + 1 + t, :] * cw[2:3])
            y = _silu(y)
            if kind == "q":
                y = y * q_scale
        else:
            y = jnp.dot(a_ref[HALO_ROWS:HALO_ROWS + t, :], w_ref[:, cols], preferred_element_type=F32)
            if kind == "sig":
                y = _sigmoid(y)
        if dest == "tr":
            tr_refs[out_tr][0] = y.T.astype(BF16)
            out_tr += 1
        else:
            proj_ref[0, :, out_col * d:(out_col + 1) * d] = y.astype(BF16)
            out_col += 1

    gates = jnp.dot(a_ref[HALO_ROWS:HALO_ROWS + t, :], wg_ref[...], preferred_element_type=F32) + bg_ref[...]
    gates_ref[0] = gates.T[0:N_GATE_COLS, :]


def _in_proj(x, mod, mod_cols, gain, w, wg, conv_w, conv_b, bg, kinds, q_scale):
    b, n, d = x.shape
    t = min(TOKEN_TILE, n)
    nt = n // t
    per_batch = mod.shape[0] > 1
    sh_col, sc_col = mod_cols

    def mod_map(col):
        return lambda bi, i: (bi if per_batch else 0, 0, col)

    t8 = t // 8
    last8 = n // 8 - 1
    in_specs = [
        pl.BlockSpec((1, 8, d), lambda bi, i: (bi, jnp.maximum(i * t8 - 1, 0), 0)),
        pl.BlockSpec((1, t, d), lambda bi, i: (bi, i, 0)),
        pl.BlockSpec((1, 8, d), lambda bi, i: (bi, jnp.minimum((i + 1) * t8, last8), 0)),
        pl.BlockSpec((1, 1, d), mod_map(sh_col)),
        pl.BlockSpec((1, 1, d), mod_map(sc_col)),
        _resident(gain.shape), _resident((d, len(kinds) * d)), _resident(wg.shape),
        _resident(conv_w.shape), _resident(conv_b.shape), _resident(bg.shape),
    ]
    n_tr = sum(dest == "tr" for _, dest in kinds)
    ncol = sum(kind != "skip" and dest == "proj" for kind, dest in kinds) * d
    return pl.pallas_call(
        functools.partial(_in_proj_kernel, kinds, q_scale),
        grid=(b, nt),
        in_specs=in_specs,
        out_specs=([pl.BlockSpec((1, t, ncol), lambda bi, i: (bi, i, 0))]
                   + [pl.BlockSpec((1, d, t), lambda bi, i: (bi, 0, i))] * n_tr
                   + [pl.BlockSpec((1, N_GATE_COLS, t), lambda bi, i: (bi, 0, i))]),
        out_shape=([jax.ShapeDtypeStruct((b, n, ncol), BF16)]
                   + [jax.ShapeDtypeStruct((b, d, n), BF16)] * n_tr
                   + [jax.ShapeDtypeStruct((b, N_GATE_COLS, n), F32)]),
        scratch_shapes=[pltpu.VMEM((t + 2 * HALO_ROWS, d), BF16),
                        pltpu.VMEM((t + 2 * HALO_ROWS, d), F32)],
        compiler_params=_params(("arbitrary", "arbitrary")),
        name="in_proj",
    )(x, x, x, mod, mod, gain, w, wg, conv_w, conv_b, bg)


GROW_BLOCKS = 5
GROW_ROWS = GROW_BLOCKS * N_GATE_COLS


def _gate_direction(x, m_init, l, backward):
    n = x.shape[1]
    nc = n // l
    pos = lax.broadcasted_iota(jnp.int32, x.shape, 1) & (l - 1)
    neg = jnp.float32(-jnp.inf)

    def shifted(v, s, fill):
        if backward:
            return jnp.where(pos < l - s, pltpu.roll(v, n - s, axis=1), fill)
        return jnp.where(pos >= s, pltpu.roll(v, s, axis=1), fill)

    csum = jnp.minimum(x, 0.0) - jnp.log1p(jnp.exp(-jnp.abs(x)))
    s = 1
    while s < l:
        csum = csum + shifted(csum, s, 0.0)
        s *= 2
    g = pltpu.roll(x, 4, axis=0) - csum
    gmax = g
    s = 1
    while s < l:
        gmax = jnp.maximum(gmax, shifted(gmax, s, neg))
        s *= 2

    m0_parts, ml_parts = [None] * nc, [None] * nc
    m0 = m_init
    for step in range(nc):
        ci = nc - 1 - step if backward else step
        last = ci * l if backward else ci * l + l - 1
        m_last = jnp.maximum(m0, gmax[:, last:last + 1])
        m0_parts[ci] = jnp.broadcast_to(m0, (8, l))
        ml_parts[ci] = jnp.broadcast_to(m_last, (8, l))
        m0 = csum[:, last:last + 1] + m_last
    m0_arr = jnp.concatenate(m0_parts, axis=1)
    ml_arr = jnp.concatenate(ml_parts, axis=1)
    big_m = jnp.maximum(m0_arr, gmax)
    f_rows = lax.broadcasted_iota(jnp.int32, x.shape, 0) >= 4
    rows = (big_m * LOG2E, jnp.exp(-(csum + big_m)), jnp.exp(g - ml_arr), m0_arr * LOG2E,
            jnp.exp(m0_arr - ml_arr))
    rows = tuple(jnp.where(f_rows, v, 0.0) for v in rows)
    return rows, jnp.where(f_rows, g * LOG2E, 0.0), m0


def _gateprep_kernel(l, has_init, *refs):
    refs = list(refs)
    gates_ref = refs.pop(0)
    m_init_ref = refs.pop(0) if has_init else None
    gcol_ref, grow_ref, m_out_ref = refs
    n = gates_ref.shape[2]
    col_blocks = []
    for d in range(2):
        rows8 = slice(8 * d, 8 * d + 8)
        m_init = m_init_ref[0, rows8, 0:1] if has_init else jnp.zeros((8, 1), F32)
        rows, g2, m_fin = _gate_direction(gates_ref[0, rows8, :], m_init, l, backward=d == 1)
        for k, v in enumerate(rows):
            grow_ref[0, k * N_GATE_COLS + 8 * d:k * N_GATE_COLS + 8 * d + 8, :] = v
        col_blocks.append(g2)
        m_out_ref[0, rows8, :] = jnp.broadcast_to(m_fin, (8, GATE_LANES))
    col_src = jnp.concatenate(col_blocks + [jnp.zeros((GATE_LANES - N_GATE_COLS, n), F32)], axis=0)
    gcol_ref[0] = col_src.T


def _gateprep(gates, l, m_init):
    b, _, n = gates.shape
    assert l & (l - 1) == 0 and n % l == 0
    in_specs = [pl.BlockSpec((1, N_GATE_COLS, n), lambda bi: (bi, 0, 0))]
    args = [gates]
    if m_init is not None:
        in_specs.append(pl.BlockSpec((1, N_GATE_COLS, GATE_LANES), lambda bi: (bi, 0, 0)))
        args.append(m_init)
    return pl.pallas_call(
        functools.partial(_gateprep_kernel, l, m_init is not None),
        grid=(b,),
        in_specs=in_specs,
        out_specs=[pl.BlockSpec((1, n, GATE_LANES), lambda bi: (bi, 0, 0)),
                   pl.BlockSpec((1, GROW_ROWS, n), lambda bi: (bi, 0, 0)),
                   pl.BlockSpec((1, N_GATE_COLS, GATE_LANES), lambda bi: (bi, 0, 0))],
        out_shape=[jax.ShapeDtypeStruct((b, n, GATE_LANES), F32),
                   jax.ShapeDtypeStruct((b, GROW_ROWS, n), F32),
                   jax.ShapeDtypeStruct((b, N_GATE_COLS, GATE_LANES), F32)],
        compiler_params=_params(("arbitrary",)),
        name="gateprep",
    )(*args)


STATE_PAD = 16


def _mlstm_kernel(has_init, emit_h, emit_state, *refs):
    refs = list(refs)
    data = [refs[0:5], refs[5:10]]
    pos = 10
    if has_init:
        s_init_ref = refs[pos]
        pos += 1
    if emit_h:
        ht_refs = refs[pos:pos + 2]
        pos += 2
    if emit_state:
        s_out_ref = refs[pos]
        pos += 1
    st_s, stb_s = refs[pos:pos + 2]

    j = pl.program_id(1)
    nj = pl.num_programs(1)
    l = data[0][1].shape[1]
    dh = data[0][1].shape[2] // M_HEADS

    @pl.when(j == 0)
    def _init():
        if has_init:
            st_s[...] = s_init_ref[0]
            stb_s[...] = s_init_ref[0].astype(BF16)
        else:
            st_s[...] = jnp.zeros_like(st_s)
            stb_s[...] = jnp.zeros_like(stb_s)

    r = lax.broadcasted_iota(jnp.int32, (l, l), 0)
    c = lax.broadcasted_iota(jnp.int32, (l, l), 1)
    neg = jnp.float32(-jnp.inf)
    ones_pad = jnp.ones((STATE_PAD, l), BF16)

    chains = [(d, h) for d in range(2) for h in range(M_HEADS)]
    visible = [r <= c, r >= c]
    gcs = [data[d][3][0] for d in range(2)]

    def qt_of(d, h):
        return data[d][0][0, h * dh:(h + 1) * dh, :]

    def k_of(d, h):
        return data[d][1][0, :, h * dh:(h + 1) * dh]

    def vt_of(d, h):
        return data[d][2][0, h * dh:(h + 1) * dh, :]

    def g_col(d, h):
        row = 8 * d + 4 + h
        return gcs[d][:, row:row + 1]

    def grow(block, d, h):
        row = block * N_GATE_COLS + 8 * d + 4 + h
        return data[d][4][0, row:row + 1, :]

    if emit_h:
        qk_t = [jnp.dot(k_of(d, h), qt_of(d, h), preferred_element_type=F32) for d, h in chains]
        inter = [jnp.dot(stb_s[d * M_HEADS + h], qt_of(d, h), preferred_element_type=F32) for d, h in chains]
        s_t = [(qk_t[i] * jnp.exp2(jnp.where(visible[d], g_col(d, h) - grow(0, d, h), neg))).astype(BF16)
               for i, (d, h) in enumerate(chains)]
        intra = [jnp.dot(jnp.concatenate([vt_of(d, h), ones_pad], axis=0), s_t[i], preferred_element_type=F32)
                 for i, (d, h) in enumerate(chains)]
    upd = []
    for d, h in chains:
        ws_b = grow(2, d, h).astype(BF16)
        lhs = jnp.concatenate([vt_of(d, h) * ws_b, jnp.broadcast_to(ws_b, (STATE_PAD, l))], axis=0)
        upd.append(jnp.dot(lhs, k_of(d, h), preferred_element_type=F32))
    for i, (d, h) in enumerate(chains):
        if emit_h:
            m0 = grow(3, d, h)[:, 0:1]
            sa_row = jnp.exp2(m0 - grow(0, d, h))
            both = intra[i] + sa_row * inter[i]
            den = both[dh:dh + 1, :]
            ht_refs[d][0, h * dh:(h + 1) * dh, :] = (
                both[0:dh, :] * (1.0 / jnp.maximum(jnp.abs(den), grow(1, d, h))))
        s0 = grow(4, d, h)[:, 0:1]
        st_new = s0 * st_s[i] + upd[i]
        st_s[i] = st_new
        stb_s[i] = st_new.astype(BF16)

    if emit_state:
        @pl.when(j == nj - 1)
        def _emit():
            s_out_ref[0] = st_s[...]


def _mlstm(qt, proj, k_col, vt, gcol, grow, l, init, emit_h, emit_state):
    b, dm, n = qt.shape
    nc = n // l
    dh = dm // M_HEADS

    def feature_major(backward):
        return pl.BlockSpec((1, dm, l), lambda bi, j: (bi, 0, nc - 1 - j if backward else j))

    def token_major(width, col, backward):
        return pl.BlockSpec((1, l, width), lambda bi, j: (bi, nc - 1 - j if backward else j, col))

    in_specs, args = [], []
    for backward in (False, True):
        in_specs += [feature_major(backward), token_major(dm, k_col, backward), feature_major(backward),
                     token_major(GATE_LANES, 0, backward),
                     pl.BlockSpec((1, GROW_ROWS, l), lambda bi, j, bw=backward: (bi, 0, nc - 1 - j if bw else j))]
        args += [qt, proj, vt, gcol, grow]
    state_shape = (2 * M_HEADS, dh + STATE_PAD, dh)
    state_spec = pl.BlockSpec((1,) + state_shape, lambda bi, j: (bi, 0, 0, 0))
    if init is not None:
        in_specs.append(state_spec)
        args.append(init)
    out_specs, out_shape = [], []
    if emit_h:
        out_specs += [feature_major(False), feature_major(True)]
        out_shape += [jax.ShapeDtypeStruct((b, dm, n), F32)] * 2
    if emit_state:
        out_specs.append(state_spec)
        out_shape.append(jax.ShapeDtypeStruct((b,) + state_shape, F32))
    return pl.pallas_call(
        functools.partial(_mlstm_kernel, init is not None, emit_h, emit_state),
        grid=(b, nc),
        in_specs=in_specs,
        out_specs=out_specs,
        out_shape=out_shape,
        scratch_shapes=[pltpu.VMEM(state_shape, F32), pltpu.VMEM(state_shape, BF16)],
        compiler_params=_params(("arbitrary", "arbitrary")),
        name="mlstm",
    )(*args)


BAND = 256


def _band_matrices():
    t = np.arange(BAND)
    same_row = (t[:, None] // GRID_W) == (t[None, :] // GRID_W)
    ct, cs = t[:, None] % GRID_W, t[None, :] % GRID_W
    mats = [same_row & (cs >= ct - w // 2) & (cs <= ct + w // 2 - 1) for w in POOL_WINDOWS]
    return jnp.asarray(np.stack(mats).astype(np.float32), dtype=BF16)


def _pool_kernel(up_ref, uc_ref, un_ref, sg_ref, band_ref, wp_ref, ps_ref, y_ref, cs_ref):
    i = pl.program_id(1)
    nt = pl.num_programs(1)
    t = uc_ref.shape[1]
    gdim = wp_ref.shape[1]
    rt = t // GRID_W
    shift = GRID_W.bit_length() - 1
    tok = lax.broadcasted_iota(jnp.int32, (t, 1), 0)
    row = i * rt + lax.shift_right_logical(tok, shift)
    col = tok & (GRID_W - 1)
    prev_ok = i > 0
    next_ok = i < nt - 1

    for g, w in enumerate(POOL_WINDOWS):
        hw = w // 2
        cols = slice(g * gdim, (g + 1) * gdim)
        band = band_ref[g]
        lo = (rt - hw) * GRID_W // BAND * BAND
        hi = -(-((2 * rt + hw - 1) * GRID_W) // BAND) * BAND
        for b0 in range(lo, hi, BAND):
            seg, off = divmod(b0, t)
            if seg == 0:
                blk = jnp.where(prev_ok, up_ref[0, off:off + BAND, cols], 0)
            elif seg == 1:
                blk = uc_ref[0, off:off + BAND, cols]
            else:
                blk = jnp.where(next_ok, un_ref[0, off:off + BAND, cols], 0)
            cs_ref[b0:b0 + BAND, :] = jnp.dot(band, blk, preferred_element_type=F32)
        acc = cs_ref[(rt - hw) * GRID_W:(rt - hw) * GRID_W + t, :]
        for jr in range(-hw + 1, hw):
            acc = acc + cs_ref[(rt + jr) * GRID_W:(rt + jr) * GRID_W + t, :]
        cnt_r = jnp.minimum(row + hw, nt * rt) - jnp.maximum(row - hw, 0)
        cnt_c = jnp.minimum(col + hw, GRID_W) - jnp.maximum(col - hw, 0)
        inv = 1.0 / (cnt_r * cnt_c).astype(F32)
        delta = acc * inv - uc_ref[0, :, cols].astype(F32)
        po = jnp.dot(delta.astype(BF16), wp_ref[g], preferred_element_type=F32)
        y_ref[0, :, cols] = (sg_ref[0, :, cols].astype(F32) * (po * ps_ref[:, cols])).astype(BF16)


def _pool(proj, dm, u_col, sg_col, bands, w_pool, pool_scale):
    b, n, _ = proj.shape
    t = TOKEN_TILE
    assert n % t == 0 and t % BAND == 0 and BAND % GRID_W == 0 and GRID_W & (GRID_W - 1) == 0
    assert t // GRID_W >= max(POOL_WINDOWS) // 2
    nt = n // t
    gdim = dm // len(POOL_WINDOWS)
    return pl.pallas_call(
        _pool_kernel,
        grid=(b, nt),
        in_specs=[pl.BlockSpec((1, t, dm), lambda bi, i: (bi, jnp.maximum(i - 1, 0), u_col)),
                  pl.BlockSpec((1, t, dm), lambda bi, i: (bi, i, u_col)),
                  pl.BlockSpec((1, t, dm), lambda bi, i: (bi, jnp.minimum(i + 1, nt - 1), u_col)),
                  pl.BlockSpec((1, t, dm), lambda bi, i: (bi, i, sg_col)),
                  _resident(bands.shape), _resident(w_pool.shape), _resident(pool_scale.shape)],
        out_specs=pl.BlockSpec((1, t, dm), lambda bi, i: (bi, i, 0)),
        out_shape=jax.ShapeDtypeStruct((b, n, dm), BF16),
        scratch_shapes=[pltpu.VMEM((3 * t, gdim), F32)],
        compiler_params=_params(("arbitrary", "arbitrary")),
        name="pool",
    )(proj, proj, proj, proj, bands, w_pool, pool_scale)


FFN_CHUNK = 256


def _tail_kernel(x_ref, hf_ref, hb_ref, yp_ref, so_ref, sgm_ref, g1_ref, sh2_ref, sc2_ref, g2_ref,
                 mhg_ref, wo_ref, nf_ref, wi_ref, wo2_ref, nfin_ref, o_ref):
    d = x_ref.shape[2]
    dh = d // M_HEADS
    f = wo2_ref.shape[0]

    hs = hf_ref[0] + hb_ref[0]
    parts = []
    for h in range(M_HEADS):
        hh = hs[h * dh:(h + 1) * dh, :]
        parts.append(hh * lax.rsqrt(jnp.sum(hh * hh, axis=0, keepdims=True) * (1.0 / dh) + EPS))
    m_out = jnp.concatenate(parts, axis=0).T * mhg_ref[...] * so_ref[0].astype(F32)
    y = yp_ref[0].astype(F32) + sgm_ref[0].astype(F32) * m_out
    mix = jnp.dot(y.astype(BF16), wo_ref[...], preferred_element_type=F32)
    x1 = x_ref[0] + g1_ref[0] * mix

    a2 = ((x1 * _rms_scale(x1, d)) * nf_ref[...] * (1.0 + sc2_ref[0]) + sh2_ref[0]).astype(BF16)
    acc = jnp.zeros_like(x1)
    for c0 in range(0, f, FFN_CHUNK):
        gate = jnp.dot(a2, wi_ref[:, c0:c0 + FFN_CHUNK], preferred_element_type=F32)
        up = jnp.dot(a2, wi_ref[:, f + c0:f + c0 + FFN_CHUNK], preferred_element_type=F32)
        act = (_silu(gate) * up).astype(BF16)
        acc = acc + jnp.dot(act, wo2_ref[c0:c0 + FFN_CHUNK, :], preferred_element_type=F32)
    x2 = x1 + g2_ref[0] * acc
    o_ref[0] = (x2 * _rms_scale(x2, d)) * nfin_ref[...]


def _tail(x, hf, hb, ypool, proj, so_col, sgm_col, mod, mh_gain, w_out, norm_ffn, w_ffn_in, w_ffn_out,
          norm_final):
    b, n, d = x.shape
    t = min(TOKEN_TILE, n)
    assert w_ffn_out.shape[0] % FFN_CHUNK == 0

    def tile(col=0):
        return pl.BlockSpec((1, t, d), lambda bi, i: (bi, i, col))

    def mod_spec(col):
        return pl.BlockSpec((1, 1, d), lambda bi, i: (bi, 0, col))

    def feature_major():
        return pl.BlockSpec((1, d, t), lambda bi, i: (bi, 0, i))

    return pl.pallas_call(
        _tail_kernel,
        grid=(b, n // t),
        in_specs=[tile(), feature_major(), feature_major(), tile(), tile(so_col), tile(sgm_col),
                  mod_spec(2), mod_spec(3), mod_spec(4), mod_spec(5),
                  _resident(mh_gain.shape), _resident(w_out.shape), _resident(norm_ffn.shape),
                  _resident(w_ffn_in.shape), _resident(w_ffn_out.shape), _resident(norm_final.shape)],
        out_specs=tile(),
        out_shape=jax.ShapeDtypeStruct((b, n, d), F32),
        compiler_params=_params(("arbitrary", "arbitrary")),
        name="tail",
    )(x, hf, hb, ypool, proj, proj, mod, mod, mod, mod, mh_gain, w_out, norm_ffn, w_ffn_in, w_ffn_out,
      norm_final)


def kernel(x, c, ctx, c_ctx, norm_mix, norm_ffn, norm_final, w_ada, b_ada, w_in, b_gates, conv_w, conv_b,
           w_pool, pool_scale, mh_gain, w_out, w_ffn_in, w_ffn_out):
    assert w_ada.shape[0] == 1, "single-layer stack only"
    b, n, d = x.shape
    assert d % (128 * M_HEADS) == 0 and w_in.shape[2] == 7 * d + N_GATE_COLS
    dh = d // M_HEADS

    pad_rows = -(b + 1) % 8
    c_rows = jnp.concatenate([c, c_ctx[None], jnp.zeros((pad_rows, d), F32)], axis=0)
    mod_all = _ada(c_rows, w_ada[0], b_ada)
    mod = mod_all[:b, None, :]
    mod_ctx = mod_all[b:b + 1, None, :]

    w_main = _cast_bf16(w_in[0], 7 * d)
    w_gate = jnp.pad(w_in[0, :, 7 * d:], ((0, 0), (0, GATE_LANES - N_GATE_COLS))).astype(BF16)
    b_gate = jnp.pad(b_gates, ((0, 0), (0, GATE_LANES - N_GATE_COLS)))
    q_scale = dh ** -0.5

    lat_kinds = (("plain", "proj"), ("q", "tr"), ("k", "proj"), ("plain", "tr"),
                 ("sig", "proj"), ("sig", "proj"), ("sig", "proj"))
    ctx_kinds = (("skip", ""), ("q", "tr"), ("k", "proj"), ("plain", "tr"))
    proj, qt, vt, gates = _in_proj(x, mod, (0, 1), norm_mix, w_main, w_gate, conv_w[0], conv_b, b_gate,
                                   lat_kinds, q_scale)
    proj_c, qt_c, vt_c, gates_c = _in_proj(ctx, mod_ctx, (0, 1), norm_mix, w_main, w_gate, conv_w[0],
                                           conv_b, b_gate, ctx_kinds, q_scale)

    l_lat = min(SCAN_CHUNK, n)
    l_ctx = min(SCAN_CHUNK, ctx.shape[1])
    gcol_c, grow_c, m_c = _gateprep(gates_c, l_ctx, None)
    gcol, grow, _ = _gateprep(gates, l_lat, m_c)
    (state,) = _mlstm(qt_c, proj_c, 0, vt_c, gcol_c, grow_c, l_ctx, None, False, True)
    h_dirs = _mlstm(qt, proj, 1, vt, gcol, grow, l_lat, state, True, False)

    groups, gdim = w_pool.shape[1], w_pool.shape[2]
    w_pool_b = _cast_bf16(w_pool[0].reshape(groups * gdim, gdim)).reshape(groups, gdim, gdim)
    ypool = _pool(proj, d, 0, 3, _band_matrices(), w_pool_b, pool_scale)
    return _tail(x, h_dirs[0], h_dirs[1], ypool, proj, 2, 4, mod, mh_gain, _cast_bf16(w_out[0]), norm_ffn,
                 _cast_bf16(w_ffn_in[0]), _cast_bf16(w_ffn_out[0]), norm_final[None])
```

```python
import functools
import math

import numpy as np
import jax
import jax.numpy as jnp
from jax import lax
from jax.experimental import pallas as pl
from jax.experimental.pallas import tpu as pltpu

F32 = jnp.float32
BF16 = jnp.bfloat16

EPS = 1e-6
LOG2E = math.log2(math.e)
GRID_W = 64
POOL_WINDOWS = (2, 4, 8, 16)
M_HEADS = 4
CONV_W = 3
N_GATE_COLS = 4 * M_HEADS
GATE_LANES = 128
SCAN_CHUNK = 256
TOKEN_TILE = 512
HALO_ROWS = 16
VMEM_LIMIT = 60 * 1024 * 1024


def _resident(shape):
    nd = len(shape)
    return pl.BlockSpec(shape, lambda *_: (0,) * nd, pipeline_mode=pl.Buffered(1))


def _params(sem):
    return pltpu.CompilerParams(dimension_semantics=sem, vmem_limit_bytes=VMEM_LIMIT)


def _sigmoid(x):
    return 1.0 / (1.0 + jnp.exp(-x))


def _silu(x):
    return x * _sigmoid(x)


def _rms_scale(xf, d):
    return lax.rsqrt(jnp.sum(xf * xf, axis=-1, keepdims=True) * (1.0 / d) + EPS)


def _ada_kernel(c_ref, w_ref, b_ref, o_ref):
    s = _silu(c_ref[...])
    o_ref[...] = jnp.dot(s.astype(BF16), w_ref[...].astype(BF16), preferred_element_type=F32) + b_ref[...]


def _ada(c_rows, w_ada, b_ada):
    r, d = c_rows.shape
    n = w_ada.shape[1]
    return pl.pallas_call(
        _ada_kernel,
        grid=(n // d,),
        in_specs=[pl.BlockSpec((r, d), lambda j: (0, 0)),
                  pl.BlockSpec((d, d), lambda j: (0, j)),
                  pl.BlockSpec((1, d), lambda j: (0, j))],
        out_specs=pl.BlockSpec((r, d), lambda j: (0, j)),
        out_shape=jax.ShapeDtypeStruct((r, n), F32),
        compiler_params=_params(("arbitrary",)),
        name="ada",
    )(c_rows, w_ada, b_ada)


def _cast_kernel(w_ref, o_ref):
    o_ref[...] = w_ref[...].astype(BF16)


def _cast_bf16(w, ncols=None):
    k, n = w.shape
    ncols = n if ncols is None else ncols
    block = max(bw for bw in range(128, min(ncols, 1024) + 1, 128) if ncols % bw == 0)
    return pl.pallas_call(
        _cast_kernel,
        grid=(ncols // block,),
        in_specs=[pl.BlockSpec((k, block), lambda j: (0, j))],
        out_specs=pl.BlockSpec((k, block), lambda j: (0, j)),
        out_shape=jax.ShapeDtypeStruct((k, ncols), BF16),
        compiler_params=_params(("arbitrary",)),
        name="wcast",
    )(w)


def _in_proj_kernel(kinds, q_scale, xp_ref, xc_ref, xn_ref, sh_ref, sc_ref, gain_ref, w_ref, wg_ref,
                    cw_ref, cb_ref, bg_ref, *out_and_scratch):
    n_tr = sum(dest == "tr" for _, dest in kinds)
    proj_ref = out_and_scratch[0]
    tr_refs = out_and_scratch[1:1 + n_tr]
    gates_ref, a_ref, qk_ref = out_and_scratch[1 + n_tr:]
    i = pl.program_id(1)
    nt = pl.num_programs(1)
    t = xc_ref.shape[1]
    d = xc_ref.shape[2]
    gain = gain_ref[...]
    mult = 1.0 + sc_ref[0]
    shift = sh_ref[0]

    def norm_mod(xf):
        return (xf * _rms_scale(xf, d)) * gain * mult + shift

    prev = jnp.where(i > 0, norm_mod(xp_ref[0]), 0.0)
    nxt = jnp.where(i < nt - 1, norm_mod(xn_ref[0]), 0.0)
    zeros8 = jnp.zeros((HALO_ROWS - 8, d), F32)
    a_ref[0:HALO_ROWS, :] = jnp.concatenate([zeros8, prev], axis=0).astype(BF16)
    a_ref[HALO_ROWS:HALO_ROWS + t, :] = norm_mod(xc_ref[0]).astype(BF16)
    a_ref[HALO_ROWS + t:2 * HALO_ROWS + t, :] = jnp.concatenate([nxt, zeros8], axis=0).astype(BF16)

    out_col = 0
    out_tr = 0
    for g, (kind, dest) in enumerate(kinds):
        cols = slice(g * d, (g + 1) * d)
        if kind == "skip":
            continue
        if kind in ("q", "k"):
            qk_ref[...] = jnp.dot(a_ref[...], w_ref[:, cols], preferred_element_type=F32)
            c0 = 0 if kind == "q" else d
            cw = cw_ref[:, c0:c0 + d]
            y = (cb_ref[:, c0:c0 + d]
                 + qk_ref[HALO_ROWS - 1:HALO_ROWS - 1 + t, :] * cw[0:1]
                 + qk_ref[HALO_ROWS:HALO_ROWS + t, :] * cw[1:2]
                 + qk_ref[HALO_ROWS + 1:HALO_ROWS + 1 + t, :] * cw[2:3])
            y = _silu(y)
            if kind == "q":
                y = y * q_scale
        else:
            y = jnp.dot(a_ref[HALO_ROWS:HALO_ROWS + t, :], w_ref[:, cols], preferred_element_type=F32)
            if kind == "sig":
                y = _sigmoid(y)
        if dest == "tr":
            tr_refs[out_tr][0] = y.T.astype(BF16)
            out_tr += 1
        else:
            proj_ref[0, :, out_col * d:(out_col + 1) * d] = y.astype(BF16)
            out_col += 1

    gates = jnp.dot(a_ref[HALO_ROWS:HALO_ROWS + t, :], wg_ref[...], preferred_element_type=F32) + bg_ref[...]
    gates_ref[0] = gates.T[0:N_GATE_COLS, :]


def _in_proj(x, mod, mod_cols, gain, w, wg, conv_w, conv_b, bg, kinds, q_scale):
    b, n, d = x.shape
    t = min(TOKEN_TILE, n)
    nt = n // t
    per_batch = mod.shape[0] > 1
    sh_col, sc_col = mod_cols

    def mod_map(col):
        return lambda bi, i: (bi if per_batch else 0, 0, col)

    t8 = t // 8
    last8 = n // 8 - 1
    in_specs = [
        pl.BlockSpec((1, 8, d), lambda bi, i: (bi, jnp.maximum(i * t8 - 1, 0), 0)),
        pl.BlockSpec((1, t, d), lambda bi, i: (bi, i, 0)),
        pl.BlockSpec((1, 8, d), lambda bi, i: (bi, jnp.minimum((i + 1) * t8, last8), 0)),
        pl.BlockSpec((1, 1, d), mod_map(sh_col)),
        pl.BlockSpec((1, 1, d), mod_map(sc_col)),
        _resident(gain.shape), _resident((d, len(kinds) * d)), _resident(wg.shape),
        _resident(conv_w.shape), _resident(conv_b.shape), _resident(bg.shape),
    ]
    n_tr = sum(dest == "tr" for _, dest in kinds)
    ncol = sum(kind != "skip" and dest == "proj" for kind, dest in kinds) * d
    return pl.pallas_call(
        functools.partial(_in_proj_kernel, kinds, q_scale),
        grid=(b, nt),
        in_specs=in_specs,
        out_specs=([pl.BlockSpec((1, t, ncol), lambda bi, i: (bi, i, 0))]
                   + [pl.BlockSpec((1, d, t), lambda bi, i: (bi, 0, i))] * n_tr
                   + [pl.BlockSpec((1, N_GATE_COLS, t), lambda bi, i: (bi, 0, i))]),
        out_shape=([jax.ShapeDtypeStruct((b, n, ncol), BF16)]
                   + [jax.ShapeDtypeStruct((b, d, n), BF16)] * n_tr
                   + [jax.ShapeDtypeStruct((b, N_GATE_COLS, n), F32)]),
        scratch_shapes=[pltpu.VMEM((t + 2 * HALO_ROWS, d), BF16),
                        pltpu.VMEM((t + 2 * HALO_ROWS, d), F32)],
        compiler_params=_params(("arbitrary", "arbitrary")),
        name="in_proj",
    )(x, x, x, mod, mod, gain, w, wg, conv_w, conv_b, bg)


GROW_BLOCKS = 5
GROW_ROWS = GROW_BLOCKS * N_GATE_COLS


def _gate_direction(x, m_init, l, backward):
    n = x.shape[1]
    nc = n // l
    pos = lax.broadcasted_iota(jnp.int32, x.shape, 1) & (l - 1)
    neg = jnp.float32(-jnp.inf)

    def shifted(v, s, fill):
        if backward:
            return jnp.where(pos < l - s, pltpu.roll(v, n - s, axis=1), fill)
        return jnp.where(pos >= s, pltpu.roll(v, s, axis=1), fill)

    csum = jnp.minimum(x, 0.0) - jnp.log1p(jnp.exp(-jnp.abs(x)))
    s = 1
    while s < l:
        csum = csum + shifted(csum, s, 0.0)
        s *= 2
    g = pltpu.roll(x, 4, axis=0) - csum
    gmax = g
    s = 1
    while s < l:
        gmax = jnp.maximum(gmax, shifted(gmax, s, neg))
        s *= 2

    m0_parts, ml_parts = [None] * nc, [None] * nc
    m0 = m_init
    for step in range(nc):
        ci = nc - 1 - step if backward else step
        last = ci * l if backward else ci * l + l - 1
        m_last = jnp.maximum(m0, gmax[:, last:last + 1])
        m0_parts[ci] = jnp.broadcast_to(m0, (8, l))
        ml_parts[ci] = jnp.broadcast_to(m_last, (8, l))
        m0 = csum[:, last:last + 1] + m_last
    m0_arr = jnp.concatenate(m0_parts, axis=1)
    ml_arr = jnp.concatenate(ml_parts, axis=1)
    big_m = jnp.maximum(m0_arr, gmax)
    f_rows = lax.broadcasted_iota(jnp.int32, x.shape, 0) >= 4
    rows = (big_m * LOG2E, jnp.exp(-(csum + big_m)), jnp.exp(g - ml_arr), m0_arr * LOG2E,
            jnp.exp(m0_arr - ml_arr))
    rows = tuple(jnp.where(f_rows, v, 0.0) for v in rows)
    return rows, jnp.where(f_rows, g * LOG2E, 0.0), m0


def _gateprep_kernel(l, has_init, *refs):
    refs = list(refs)
    gates_ref = refs.pop(0)
    m_init_ref = refs.pop(0) if has_init else None
    gcol_ref, grow_ref, m_out_ref = refs
    n = gates_ref.shape[2]
    col_blocks = []
    for d in range(2):
        rows8 = slice(8 * d, 8 * d + 8)
        m_init = m_init_ref[0, rows8, 0:1] if has_init else jnp.zeros((8, 1), F32)
        rows, g2, m_fin = _gate_direction(gates_ref[0, rows8, :], m_init, l, backward=d == 1)
        for k, v in enumerate(rows):
            grow_ref[0, k * N_GATE_COLS + 8 * d:k * N_GATE_COLS + 8 * d + 8, :] = v
        col_blocks.append(g2)
        m_out_ref[0, rows8, :] = jnp.broadcast_to(m_fin, (8, GATE_LANES))
    col_src = jnp.concatenate(col_blocks + [jnp.zeros((GATE_LANES - N_GATE_COLS, n), F32)], axis=0)
    gcol_ref[0] = col_src.T


def _gateprep(gates, l, m_init):
    b, _, n = gates.shape
    assert l & (l - 1) == 0 and n % l == 0
    in_specs = [pl.BlockSpec((1, N_GATE_COLS, n), lambda bi: (bi, 0, 0))]
    args = [gates]
    if m_init is not None:
        in_specs.append(pl.BlockSpec((1, N_GATE_COLS, GATE_LANES), lambda bi: (bi, 0, 0)))
        args.append(m_init)
    return pl.pallas_call(
        functools.partial(_gateprep_kernel, l, m_init is not None),
        grid=(b,),
        in_specs=in_specs,
        out_specs=[pl.BlockSpec((1, n, GATE_LANES), lambda bi: (bi, 0, 0)),
                   pl.BlockSpec((1, GROW_ROWS, n), lambda bi: (bi, 0, 0)),
                   pl.BlockSpec((1, N_GATE_COLS, GATE_LANES), lambda bi: (bi, 0, 0))],
        out_shape=[jax.ShapeDtypeStruct((b, n, GATE_LANES), F32),
                   jax.ShapeDtypeStruct((b, GROW_ROWS, n), F32),
                   jax.ShapeDtypeStruct((b, N_GATE_COLS, GATE_LANES), F32)],
        compiler_params=_params(("arbitrary",)),
        name="gateprep",
    )(*args)


STATE_PAD = 16
CHAIN_GROUP = 4


def _mlstm_kernel(has_init, emit_h, emit_state, *refs):
    refs = list(refs)
    data = [refs[0:5], refs[5:10]]
    pos = 10
    if has_init:
        s_init_ref = refs[pos]
        pos += 1
    if emit_h:
        h_refs = refs[pos:pos + 2]
        pos += 2
    if emit_state:
        s_out_ref = refs[pos]
        pos += 1
    st_s, stb_s = refs[pos:pos + 2]

    j = pl.program_id(1)
    nj = pl.num_programs(1)
    l = data[0][1].shape[1]
    dh = data[0][1].shape[2] // M_HEADS

    @pl.when(j == 0)
    def _init():
        if has_init:
            st_s[...] = s_init_ref[0]
            stb_s[...] = s_init_ref[0].astype(BF16)
        else:
            st_s[...] = jnp.zeros_like(st_s)
            stb_s[...] = jnp.zeros_like(stb_s)

    r = lax.broadcasted_iota(jnp.int32, (l, l), 0)
    c = lax.broadcasted_iota(jnp.int32, (l, l), 1)
    neg = jnp.float32(-jnp.inf)
    ones_pad = jnp.ones((STATE_PAD, l), BF16)

    chains = [(d, h) for d in range(2) for h in range(M_HEADS)]
    visible = [r <= c, r >= c]
    gcs = [data[d][3][0] for d in range(2)]

    def q_of(d, h):
        return data[d][0][0, :, h * dh:(h + 1) * dh]

    def dot_nt(a, b_mat):
        return lax.dot_general(a, b_mat, (((1,), (1,)), ((), ())), preferred_element_type=F32)

    def k_of(d, h):
        return data[d][1][0, :, h * dh:(h + 1) * dh]

    def vt_of(d, h):
        return data[d][2][0, h * dh:(h + 1) * dh, :]

    def g_col(d, h):
        row = 8 * d + 4 + h
        return gcs[d][:, row:row + 1]

    def grow(block, d, h):
        row = block * N_GATE_COLS + 8 * d + 4 + h
        return data[d][4][0, row:row + 1, :]

    for g0 in range(0, len(chains), CHAIN_GROUP):
        group = chains[g0:g0 + CHAIN_GROUP]
        if emit_h:
            qk_t = [dot_nt(k_of(d, h), q_of(d, h)) for d, h in group]
            inter = [dot_nt(stb_s[d * M_HEADS + h], q_of(d, h)) for d, h in group]
            s_t = [(qk_t[i] * jnp.exp2(jnp.where(visible[d], g_col(d, h) - grow(0, d, h), neg))).astype(BF16)
                   for i, (d, h) in enumerate(group)]
            intra = [jnp.dot(jnp.concatenate([vt_of(d, h), ones_pad], axis=0), s_t[i],
                             preferred_element_type=F32) for i, (d, h) in enumerate(group)]
        upd = []
        for d, h in group:
            ws_b = grow(2, d, h).astype(BF16)
            lhs = jnp.concatenate([vt_of(d, h) * ws_b, jnp.broadcast_to(ws_b, (STATE_PAD, l))], axis=0)
            upd.append(jnp.dot(lhs, k_of(d, h), preferred_element_type=F32))
        for i, (d, h) in enumerate(group):
            if emit_h:
                m0 = grow(3, d, h)[:, 0:1]
                sa_row = jnp.exp2(m0 - grow(0, d, h))
                both = intra[i] + sa_row * inter[i]
                den = both[dh:dh + 1, :]
                h_t = both[0:dh, :] * (1.0 / jnp.maximum(jnp.abs(den), grow(1, d, h)))
                h_refs[d][0, :, h * dh:(h + 1) * dh] = h_t.astype(BF16).T
            s0 = grow(4, d, h)[:, 0:1]
            st_new = s0 * st_s[d * M_HEADS + h] + upd[i]
            st_s[d * M_HEADS + h] = st_new
            stb_s[d * M_HEADS + h] = st_new.astype(BF16)

    if emit_state:
        @pl.when(j == nj - 1)
        def _emit():
            s_out_ref[0] = st_s[...]


def _mlstm(proj, q_col, k_col, vt, gcol, grow, l, init, emit_h, emit_state):
    b, dm, n = vt.shape
    nc = n // l
    dh = dm // M_HEADS

    def feature_major(backward):
        return pl.BlockSpec((1, dm, l), lambda bi, j: (bi, 0, nc - 1 - j if backward else j))

    def token_major(width, col, backward):
        return pl.BlockSpec((1, l, width), lambda bi, j: (bi, nc - 1 - j if backward else j, col))

    in_specs, args = [], []
    for backward in (False, True):
        in_specs += [token_major(dm, q_col, backward), token_major(dm, k_col, backward), feature_major(backward),
                     token_major(GATE_LANES, 0, backward),
                     pl.BlockSpec((1, GROW_ROWS, l), lambda bi, j, bw=backward: (bi, 0, nc - 1 - j if bw else j))]
        args += [proj, proj, vt, gcol, grow]
    state_shape = (2 * M_HEADS, dh + STATE_PAD, dh)
    state_spec = pl.BlockSpec((1,) + state_shape, lambda bi, j: (bi, 0, 0, 0))
    if init is not None:
        in_specs.append(state_spec)
        args.append(init)
    out_specs, out_shape = [], []
    if emit_h:
        out_specs += [token_major(dm, 0, False), token_major(dm, 0, True)]
        out_shape += [jax.ShapeDtypeStruct((b, n, dm), BF16)] * 2
    if emit_state:
        out_specs.append(state_spec)
        out_shape.append(jax.ShapeDtypeStruct((b,) + state_shape, F32))
    return pl.pallas_call(
        functools.partial(_mlstm_kernel, init is not None, emit_h, emit_state),
        grid=(b, nc),
        in_specs=in_specs,
        out_specs=out_specs,
        out_shape=out_shape,
        scratch_shapes=[pltpu.VMEM(state_shape, F32), pltpu.VMEM(state_shape, BF16)],
        compiler_params=_params(("arbitrary", "arbitrary")),
        name="mlstm",
    )(*args)


BAND = 256


def _band_matrices():
    t = np.arange(BAND)
    same_row = (t[:, None] // GRID_W) == (t[None, :] // GRID_W)
    ct, cs = t[:, None] % GRID_W, t[None, :] % GRID_W
    mats = [same_row & (cs >= ct - w // 2) & (cs <= ct + w // 2 - 1) for w in POOL_WINDOWS]
    return jnp.asarray(np.stack(mats).astype(np.float32), dtype=BF16)


def _pool_kernel(up_ref, uc_ref, un_ref, sg_ref, band_ref, wp_ref, ps_ref, y_ref, cs_ref):
    i = pl.program_id(1)
    nt = pl.num_programs(1)
    t = uc_ref.shape[1]
    gdim = wp_ref.shape[1]
    rt = t // GRID_W
    shift = GRID_W.bit_length() - 1
    tok = lax.broadcasted_iota(jnp.int32, (t, 1), 0)
    row = i * rt + lax.shift_right_logical(tok, shift)
    col = tok & (GRID_W - 1)
    prev_ok = i > 0
    next_ok = i < nt - 1

    for g, w in enumerate(POOL_WINDOWS):
        hw = w // 2
        cols = slice(g * gdim, (g + 1) * gdim)
        band = band_ref[g]
        lo = (rt - hw) * GRID_W // BAND * BAND
        hi = -(-((2 * rt + hw - 1) * GRID_W) // BAND) * BAND
        for b0 in range(lo, hi, BAND):
            seg, off = divmod(b0, t)
            if seg == 0:
                blk = jnp.where(prev_ok, up_ref[0, off:off + BAND, cols], 0)
            elif seg == 1:
                blk = uc_ref[0, off:off + BAND, cols]
            else:
                blk = jnp.where(next_ok, un_ref[0, off:off + BAND, cols], 0)
            cs_ref[b0:b0 + BAND, :] = jnp.dot(band, blk, preferred_element_type=F32)
        acc = cs_ref[(rt - hw) * GRID_W:(rt - hw) * GRID_W + t, :]
        for jr in range(-hw + 1, hw):
            acc = acc + cs_ref[(rt + jr) * GRID_W:(rt + jr) * GRID_W + t, :]
        cnt_r = jnp.minimum(row + hw, nt * rt) - jnp.maximum(row - hw, 0)
        cnt_c = jnp.minimum(col + hw, GRID_W) - jnp.maximum(col - hw, 0)
        inv = 1.0 / (cnt_r * cnt_c).astype(F32)
        delta = acc * inv - uc_ref[0, :, cols].astype(F32)
        po = jnp.dot(delta.astype(BF16), wp_ref[g], preferred_element_type=F32)
        y_ref[0, :, cols] = (sg_ref[0, :, cols].astype(F32) * (po * ps_ref[:, cols])).astype(BF16)


def _pool(proj, dm, u_col, sg_col, bands, w_pool, pool_scale):
    b, n, _ = proj.shape
    t = TOKEN_TILE
    assert n % t == 0 and t % BAND == 0 and BAND % GRID_W == 0 and GRID_W & (GRID_W - 1) == 0
    assert t // GRID_W >= max(POOL_WINDOWS) // 2
    nt = n // t
    gdim = dm // len(POOL_WINDOWS)
    return pl.pallas_call(
        _pool_kernel,
        grid=(b, nt),
        in_specs=[pl.BlockSpec((1, t, dm), lambda bi, i: (bi, jnp.maximum(i - 1, 0), u_col)),
                  pl.BlockSpec((1, t, dm), lambda bi, i: (bi, i, u_col)),
                  pl.BlockSpec((1, t, dm), lambda bi, i: (bi, jnp.minimum(i + 1, nt - 1), u_col)),
                  pl.BlockSpec((1, t, dm), lambda bi, i: (bi, i, sg_col)),
                  _resident(bands.shape), _resident(w_pool.shape), _resident(pool_scale.shape)],
        out_specs=pl.BlockSpec((1, t, dm), lambda bi, i: (bi, i, 0)),
        out_shape=jax.ShapeDtypeStruct((b, n, dm), BF16),
        scratch_shapes=[pltpu.VMEM((3 * t, gdim), F32)],
        compiler_params=_params(("arbitrary", "arbitrary")),
        name="pool",
    )(proj, proj, proj, proj, bands, w_pool, pool_scale)


FFN_CHUNK = 256


def _tail_kernel(x_ref, hf_ref, hb_ref, yp_ref, so_ref, sgm_ref, g1_ref, sh2_ref, sc2_ref, g2_ref,
                 mhg_ref, wo_ref, nf_ref, wi_ref, wo2_ref, nfin_ref, o_ref):
    d = x_ref.shape[2]
    dh = d // M_HEADS
    f = wo2_ref.shape[0]

    hs = hf_ref[0].astype(F32) + hb_ref[0].astype(F32)
    parts = []
    for h in range(M_HEADS):
        hh = hs[:, h * dh:(h + 1) * dh]
        parts.append(hh * _rms_scale(hh, dh))
    m_out = jnp.concatenate(parts, axis=-1) * mhg_ref[...] * so_ref[0].astype(F32)
    y = yp_ref[0].astype(F32) + sgm_ref[0].astype(F32) * m_out
    mix = jnp.dot(y.astype(BF16), wo_ref[...], preferred_element_type=F32)
    x1 = x_ref[0] + g1_ref[0] * mix

    a2 = ((x1 * _rms_scale(x1, d)) * nf_ref[...] * (1.0 + sc2_ref[0]) + sh2_ref[0]).astype(BF16)
    acc = jnp.zeros_like(x1)
    for c0 in range(0, f, FFN_CHUNK):
        gate = jnp.dot(a2, wi_ref[:, c0:c0 + FFN_CHUNK], preferred_element_type=F32)
        up = jnp.dot(a2, wi_ref[:, f + c0:f + c0 + FFN_CHUNK], preferred_element_type=F32)
        act = (_silu(gate) * up).astype(BF16)
        acc = acc + jnp.dot(act, wo2_ref[c0:c0 + FFN_CHUNK, :], preferred_element_type=F32)
    x2 = x1 + g2_ref[0] * acc
    o_ref[0] = (x2 * _rms_scale(x2, d)) * nfin_ref[...]


def _tail(x, hf, hb, ypool, proj, so_col, sgm_col, mod, mh_gain, w_out, norm_ffn, w_ffn_in, w_ffn_out,
          norm_final):
    b, n, d = x.shape
    t = min(TOKEN_TILE, n)
    assert w_ffn_out.shape[0] % FFN_CHUNK == 0

    def tile(col=0):
        return pl.BlockSpec((1, t, d), lambda bi, i: (bi, i, col))

    def mod_spec(col):
        return pl.BlockSpec((1, 1, d), lambda bi, i: (bi, 0, col))

    return pl.pallas_call(
        _tail_kernel,
        grid=(b, n // t),
        in_specs=[tile(), tile(), tile(), tile(), tile(so_col), tile(sgm_col),
                  mod_spec(2), mod_spec(3), mod_spec(4), mod_spec(5),
                  _resident(mh_gain.shape), _resident(w_out.shape), _resident(norm_ffn.shape),
                  _resident(w_ffn_in.shape), _resident(w_ffn_out.shape), _resident(norm_final.shape)],
        out_specs=tile(),
        out_shape=jax.ShapeDtypeStruct((b, n, d), F32),
        compiler_params=_params(("arbitrary", "arbitrary")),
        name="tail",
    )(x, hf, hb, ypool, proj, proj, mod, mod, mod, mod, mh_gain, w_out, norm_ffn, w_ffn_in, w_ffn_out,
      norm_final)


def kernel(x, c, ctx, c_ctx, norm_mix, norm_ffn, norm_final, w_ada, b_ada, w_in, b_gates, conv_w, conv_b,
           w_pool, pool_scale, mh_gain, w_out, w_ffn_in, w_ffn_out):
    assert w_ada.shape[0] == 1, "single-layer stack only"
    b, n, d = x.shape
    assert d % (128 * M_HEADS) == 0 and w_in.shape[2] == 7 * d + N_GATE_COLS
    dh = d // M_HEADS

    pad_rows = -(b + 1) % 8
    c_rows = jnp.concatenate([c, c_ctx[None], jnp.zeros((pad_rows, d), F32)], axis=0)
    mod_all = _ada(c_rows, w_ada[0], b_ada)
    mod = mod_all[:b, None, :]
    mod_ctx = mod_all[b:b + 1, None, :]

    w_main = _cast_bf16(w_in[0], 7 * d)
    w_gate = jnp.pad(w_in[0, :, 7 * d:], ((0, 0), (0, GATE_LANES - N_GATE_COLS))).astype(BF16)
    b_gate = jnp.pad(b_gates, ((0, 0), (0, GATE_LANES - N_GATE_COLS)))
    q_scale = dh ** -0.5

    lat_kinds = (("plain", "proj"), ("q", "proj"), ("k", "proj"), ("plain", "tr"),
                 ("sig", "proj"), ("sig", "proj"), ("sig", "proj"))
    ctx_kinds = (("skip", ""), ("q", "proj"), ("k", "proj"), ("plain", "tr"))
    proj, vt, gates = _in_proj(x, mod, (0, 1), norm_mix, w_main, w_gate, conv_w[0], conv_b, b_gate,
                               lat_kinds, q_scale)
    proj_c, vt_c, gates_c = _in_proj(ctx, mod_ctx, (0, 1), norm_mix, w_main, w_gate, conv_w[0],
                                     conv_b, b_gate, ctx_kinds, q_scale)

    l_lat = min(SCAN_CHUNK, n)
    l_ctx = min(SCAN_CHUNK, ctx.shape[1])
    gcol_c, grow_c, m_c = _gateprep(gates_c, l_ctx, None)
    gcol, grow, _ = _gateprep(gates, l_lat, m_c)
    (state,) = _mlstm(proj_c, 0, 1, vt_c, gcol_c, grow_c, l_ctx, None, False, True)
    h_dirs = _mlstm(proj, 1, 2, vt, gcol, grow, l_lat, state, True, False)

    groups, gdim = w_pool.shape[1], w_pool.shape[2]
    w_pool_b = _cast_bf16(w_pool[0].reshape(groups * gdim, gdim)).reshape(groups, gdim, gdim)
    ypool = _pool(proj, d, 0, 4, _band_matrices(), w_pool_b, pool_scale)
    return _tail(x, h_dirs[0], h_dirs[1], ypool, proj, 3, 5, mod, mh_gain, _cast_bf16(w_out[0]), norm_ffn,
                 _cast_bf16(w_ffn_in[0]), _cast_bf16(w_ffn_out[0]), norm_final[None])
```

```python
import functools
import math

import numpy as np
import jax
import jax.numpy as jnp
from jax import lax
from jax.experimental import pallas as pl
from jax.experimental.pallas import tpu as pltpu

F32 = jnp.float32
BF16 = jnp.bfloat16

EPS = 1e-6
LOG2E = math.log2(math.e)
GRID_W = 64
POOL_WINDOWS = (2, 4, 8, 16)
M_HEADS = 4
CONV_W = 3
N_GATE_COLS = 4 * M_HEADS
GATE_LANES = 128
SCAN_CHUNK = 256
TOKEN_TILE = 512
HALO_ROWS = 16
VMEM_LIMIT = 60 * 1024 * 1024


def _resident(shape):
    nd = len(shape)
    return pl.BlockSpec(shape, lambda *_: (0,) * nd, pipeline_mode=pl.Buffered(1))


def _params(sem):
    return pltpu.CompilerParams(dimension_semantics=sem, vmem_limit_bytes=VMEM_LIMIT)


def _sigmoid(x):
    return 1.0 / (1.0 + jnp.exp(-x))


def _silu(x):
    return x * _sigmoid(x)


def _rms_scale(xf, d):
    return lax.rsqrt(jnp.sum(xf * xf, axis=-1, keepdims=True) * (1.0 / d) + EPS)


def _ada_kernel(c_ref, w_ref, b_ref, o_ref):
    s = _silu(c_ref[...])
    o_ref[...] = jnp.dot(s.astype(BF16), w_ref[...].astype(BF16), preferred_element_type=F32) + b_ref[...]


def _ada(c_rows, w_ada, b_ada):
    r, d = c_rows.shape
    n = w_ada.shape[1]
    return pl.pallas_call(
        _ada_kernel,
        grid=(n // d,),
        in_specs=[pl.BlockSpec((r, d), lambda j: (0, 0)),
                  pl.BlockSpec((d, d), lambda j: (0, j)),
                  pl.BlockSpec((1, d), lambda j: (0, j))],
        out_specs=pl.BlockSpec((r, d), lambda j: (0, j)),
        out_shape=jax.ShapeDtypeStruct((r, n), F32),
        compiler_params=_params(("arbitrary",)),
        name="ada",
    )(c_rows, w_ada, b_ada)


def _cast_kernel(w_ref, o_ref):
    o_ref[...] = w_ref[...].astype(BF16)


def _cast_bf16(w, ncols=None):
    k, n = w.shape
    ncols = n if ncols is None else ncols
    block = max(bw for bw in range(128, min(ncols, 1024) + 1, 128) if ncols % bw == 0)
    return pl.pallas_call(
        _cast_kernel,
        grid=(ncols // block,),
        in_specs=[pl.BlockSpec((k, block), lambda j: (0, j))],
        out_specs=pl.BlockSpec((k, block), lambda j: (0, j)),
        out_shape=jax.ShapeDtypeStruct((k, ncols), BF16),
        compiler_params=_params(("arbitrary",)),
        name="wcast",
    )(w)


def _in_proj_kernel(kinds, q_scale, with_pool, xp_ref, xc_ref, xn_ref, sh_ref, sc_ref, gain_ref, w_ref, wg_ref,
                    cw_ref, cb_ref, bg_ref, *rest):
    rest = list(rest)
    if with_pool:
        band_ref, wp_ref, ps_ref = rest[:3]
        rest = rest[3:]
    n_tr = sum(dest == "tr" for _, dest in kinds)
    proj_ref = rest[0]
    tr_refs = rest[1:1 + n_tr]
    gates_ref = rest[1 + n_tr]
    rest = rest[2 + n_tr:]
    if with_pool:
        ypool_ref, a_ref, qk_ref, u_ring, sg_ring, cs_ref = rest
    else:
        a_ref, qk_ref = rest
    i = pl.program_id(1)
    nt = pl.num_programs(1) - (1 if with_pool else 0)
    t = xc_ref.shape[1]
    d = xc_ref.shape[2]

    def project():
        gain = gain_ref[...]
        mult = 1.0 + sc_ref[0]
        shift = sh_ref[0]

        def norm_mod(xf):
            return (xf * _rms_scale(xf, d)) * gain * mult + shift

        prev = jnp.where(i > 0, norm_mod(xp_ref[0]), 0.0)
        nxt = jnp.where(i < nt - 1, norm_mod(xn_ref[0]), 0.0)
        zeros8 = jnp.zeros((HALO_ROWS - 8, d), F32)
        a_ref[0:HALO_ROWS, :] = jnp.concatenate([zeros8, prev], axis=0).astype(BF16)
        a_ref[HALO_ROWS:HALO_ROWS + t, :] = norm_mod(xc_ref[0]).astype(BF16)
        a_ref[HALO_ROWS + t:2 * HALO_ROWS + t, :] = jnp.concatenate([nxt, zeros8], axis=0).astype(BF16)

        out_col = 0
        out_tr = 0
        for g, (kind, dest) in enumerate(kinds):
            cols = slice(g * d, (g + 1) * d)
            if kind == "skip":
                continue
            if kind in ("q", "k"):
                qk_ref[...] = jnp.dot(a_ref[...], w_ref[:, cols], preferred_element_type=F32)
                c0 = 0 if kind == "q" else d
                cw = cw_ref[:, c0:c0 + d]
                y = (cb_ref[:, c0:c0 + d]
                     + qk_ref[HALO_ROWS - 1:HALO_ROWS - 1 + t, :] * cw[0:1]
                     + qk_ref[HALO_ROWS:HALO_ROWS + t, :] * cw[1:2]
                     + qk_ref[HALO_ROWS + 1:HALO_ROWS + 1 + t, :] * cw[2:3])
                y = _silu(y)
                if kind == "q":
                    y = y * q_scale
            else:
                y = jnp.dot(a_ref[HALO_ROWS:HALO_ROWS + t, :], w_ref[:, cols], preferred_element_type=F32)
                if kind == "sig":
                    y = _sigmoid(y)
            if dest == "tr":
                tr_refs[out_tr][0] = y.T.astype(BF16)
                out_tr += 1
            elif dest == "pool_u":
                u_ring[lax.rem(i, POOL_RING)] = y.astype(BF16)
            elif dest == "pool_gate":
                sg_ring[lax.rem(i, 2)] = y.astype(BF16)
            else:
                proj_ref[0, :, out_col * d:(out_col + 1) * d] = y.astype(BF16)
                out_col += 1

        gates = jnp.dot(a_ref[HALO_ROWS:HALO_ROWS + t, :], wg_ref[...], preferred_element_type=F32) + bg_ref[...]
        gates_ref[0] = gates.T[0:N_GATE_COLS, :]

    if not with_pool:
        project()
        return

    def pool():
        p = i - 1
        u_tiles = [u_ring.at[lax.rem(p + POOL_RING - 1, POOL_RING)], u_ring.at[lax.rem(p, POOL_RING)],
                   u_ring.at[lax.rem(p + 1, POOL_RING)]]
        _pool_tile(u_tiles, sg_ring.at[lax.rem(p, 2)], band_ref, wp_ref, ps_ref, ypool_ref, cs_ref,
                   p, nt, p >= 1, p < nt - 1)

    pl.when(i < nt)(project)
    pl.when(i >= 1)(pool)


def _in_proj(x, mod, mod_cols, gain, w, wg, conv_w, conv_b, bg, kinds, q_scale, pool=None):
    b, n, d = x.shape
    t = min(TOKEN_TILE, n)
    nt = n // t
    per_batch = mod.shape[0] > 1
    sh_col, sc_col = mod_cols
    with_pool = pool is not None

    def tile_of(i):
        return jnp.minimum(i, nt - 1)

    def mod_map(col):
        return lambda bi, i: (bi if per_batch else 0, 0, col)

    t8 = t // 8
    last8 = n // 8 - 1
    in_specs = [
        pl.BlockSpec((1, 8, d), lambda bi, i: (bi, jnp.maximum(tile_of(i) * t8 - 1, 0), 0)),
        pl.BlockSpec((1, t, d), lambda bi, i: (bi, tile_of(i), 0)),
        pl.BlockSpec((1, 8, d), lambda bi, i: (bi, jnp.minimum((tile_of(i) + 1) * t8, last8), 0)),
        pl.BlockSpec((1, 1, d), mod_map(sh_col)),
        pl.BlockSpec((1, 1, d), mod_map(sc_col)),
        _resident(gain.shape), _resident((d, len(kinds) * d)), _resident(wg.shape),
        _resident(conv_w.shape), _resident(conv_b.shape), _resident(bg.shape),
    ]
    args = [x, x, x, mod, mod, gain, w, wg, conv_w, conv_b, bg]
    n_tr = sum(dest == "tr" for _, dest in kinds)
    ncol = sum(kind != "skip" and dest == "proj" for kind, dest in kinds) * d
    out_specs = ([pl.BlockSpec((1, t, ncol), lambda bi, i: (bi, tile_of(i), 0))]
                 + [pl.BlockSpec((1, d, t), lambda bi, i: (bi, 0, tile_of(i)))] * n_tr
                 + [pl.BlockSpec((1, N_GATE_COLS, t), lambda bi, i: (bi, 0, tile_of(i)))])
    out_shape = ([jax.ShapeDtypeStruct((b, n, ncol), BF16)]
                 + [jax.ShapeDtypeStruct((b, d, n), BF16)] * n_tr
                 + [jax.ShapeDtypeStruct((b, N_GATE_COLS, n), F32)])
    scratch = [pltpu.VMEM((t + 2 * HALO_ROWS, d), BF16), pltpu.VMEM((t + 2 * HALO_ROWS, d), F32)]
    if with_pool:
        assert t == TOKEN_TILE and n % t == 0 and t % BAND == 0 and BAND % GRID_W == 0
        assert GRID_W & (GRID_W - 1) == 0 and t // GRID_W >= max(POOL_WINDOWS) // 2
        in_specs += [_resident(a.shape) for a in pool]
        args += list(pool)
        out_specs.append(pl.BlockSpec((1, t, d), lambda bi, i: (bi, jnp.maximum(i - 1, 0), 0)))
        out_shape.append(jax.ShapeDtypeStruct((b, n, d), BF16))
        scratch += [pltpu.VMEM((POOL_RING, t, d), BF16), pltpu.VMEM((2, t, d), BF16),
                    pltpu.VMEM((3 * t, d // len(POOL_WINDOWS)), F32)]
    return pl.pallas_call(
        functools.partial(_in_proj_kernel, kinds, q_scale, with_pool),
        grid=(b, nt + 1 if with_pool else nt),
        in_specs=in_specs,
        out_specs=out_specs,
        out_shape=out_shape,
        scratch_shapes=scratch,
        compiler_params=_params(("arbitrary", "arbitrary")),
        name="in_proj",
    )(*args)


GROW_BLOCKS = 5
GROW_ROWS = GROW_BLOCKS * N_GATE_COLS


def _gate_direction(x, m_init, l, backward):
    n = x.shape[1]
    nc = n // l
    pos = lax.broadcasted_iota(jnp.int32, x.shape, 1) & (l - 1)
    neg = jnp.float32(-jnp.inf)

    def shifted(v, s, fill):
        if backward:
            return jnp.where(pos < l - s, pltpu.roll(v, n - s, axis=1), fill)
        return jnp.where(pos >= s, pltpu.roll(v, s, axis=1), fill)

    csum = jnp.minimum(x, 0.0) - jnp.log1p(jnp.exp(-jnp.abs(x)))
    s = 1
    while s < l:
        csum = csum + shifted(csum, s, 0.0)
        s *= 2
    g = pltpu.roll(x, 4, axis=0) - csum
    gmax = g
    s = 1
    while s < l:
        gmax = jnp.maximum(gmax, shifted(gmax, s, neg))
        s *= 2

    m0_parts, ml_parts = [None] * nc, [None] * nc
    m0 = m_init
    for step in range(nc):
        ci = nc - 1 - step if backward else step
        last = ci * l if backward else ci * l + l - 1
        m_last = jnp.maximum(m0, gmax[:, last:last + 1])
        m0_parts[ci] = jnp.broadcast_to(m0, (8, l))
        ml_parts[ci] = jnp.broadcast_to(m_last, (8, l))
        m0 = csum[:, last:last + 1] + m_last
    m0_arr = jnp.concatenate(m0_parts, axis=1)
    ml_arr = jnp.concatenate(ml_parts, axis=1)
    big_m = jnp.maximum(m0_arr, gmax)
    f_rows = lax.broadcasted_iota(jnp.int32, x.shape, 0) >= 4
    rows = (big_m * LOG2E, jnp.exp(-(csum + big_m)), jnp.exp(g - ml_arr), m0_arr * LOG2E,
            jnp.exp(m0_arr - ml_arr))
    rows = tuple(jnp.where(f_rows, v, 0.0) for v in rows)
    return rows, jnp.where(f_rows, g * LOG2E, 0.0), m0


def _gateprep_kernel(l, has_init, *refs):
    refs = list(refs)
    gates_ref = refs.pop(0)
    m_init_ref = refs.pop(0) if has_init else None
    gcol_ref, grow_ref, m_out_ref = refs
    n = gates_ref.shape[2]
    col_blocks = []
    for d in range(2):
        rows8 = slice(8 * d, 8 * d + 8)
        m_init = m_init_ref[0, rows8, 0:1] if has_init else jnp.zeros((8, 1), F32)
        rows, g2, m_fin = _gate_direction(gates_ref[0, rows8, :], m_init, l, backward=d == 1)
        for k, v in enumerate(rows):
            grow_ref[0, k * N_GATE_COLS + 8 * d:k * N_GATE_COLS + 8 * d + 8, :] = v
        col_blocks.append(g2)
        m_out_ref[0, rows8, :] = jnp.broadcast_to(m_fin, (8, GATE_LANES))
    col_src = jnp.concatenate(col_blocks + [jnp.zeros((GATE_LANES - N_GATE_COLS, n), F32)], axis=0)
    gcol_ref[0] = col_src.T


def _gateprep(gates, l, m_init):
    b, _, n = gates.shape
    assert l & (l - 1) == 0 and n % l == 0
    in_specs = [pl.BlockSpec((1, N_GATE_COLS, n), lambda bi: (bi, 0, 0))]
    args = [gates]
    if m_init is not None:
        in_specs.append(pl.BlockSpec((1, N_GATE_COLS, GATE_LANES), lambda bi: (bi, 0, 0)))
        args.append(m_init)
    return pl.pallas_call(
        functools.partial(_gateprep_kernel, l, m_init is not None),
        grid=(b,),
        in_specs=in_specs,
        out_specs=[pl.BlockSpec((1, n, GATE_LANES), lambda bi: (bi, 0, 0)),
                   pl.BlockSpec((1, GROW_ROWS, n), lambda bi: (bi, 0, 0)),
                   pl.BlockSpec((1, N_GATE_COLS, GATE_LANES), lambda bi: (bi, 0, 0))],
        out_shape=[jax.ShapeDtypeStruct((b, n, GATE_LANES), F32),
                   jax.ShapeDtypeStruct((b, GROW_ROWS, n), F32),
                   jax.ShapeDtypeStruct((b, N_GATE_COLS, GATE_LANES), F32)],
        compiler_params=_params(("arbitrary",)),
        name="gateprep",
    )(*args)


STATE_PAD = 16
CHAIN_GROUP = 2


def _mlstm_kernel(has_init, emit_h, emit_state, *refs):
    refs = list(refs)
    data = [refs[0:5], refs[5:10]]
    pos = 10
    if has_init:
        s_init_ref = refs[pos]
        pos += 1
    if emit_h:
        h_refs = refs[pos:pos + 2]
        pos += 2
    if emit_state:
        s_out_ref = refs[pos]
        pos += 1
    st_s, stb_s = refs[pos:pos + 2]

    j = pl.program_id(1)
    nj = pl.num_programs(1)
    l = data[0][1].shape[1]
    dh = data[0][1].shape[2] // M_HEADS

    @pl.when(j == 0)
    def _init():
        if has_init:
            st_s[...] = s_init_ref[0]
            stb_s[...] = s_init_ref[0].astype(BF16)
        else:
            st_s[...] = jnp.zeros_like(st_s)
            stb_s[...] = jnp.zeros_like(stb_s)

    r = lax.broadcasted_iota(jnp.int32, (l, l), 0)
    c = lax.broadcasted_iota(jnp.int32, (l, l), 1)
    neg = jnp.float32(-jnp.inf)
    ones_pad = jnp.ones((STATE_PAD, l), BF16)

    chains = [(d, h) for d in range(2) for h in range(M_HEADS)]
    visible = [r <= c, r >= c]
    gcs = [data[d][3][0] for d in range(2)]

    def q_of(d, h):
        return data[d][0][0, :, h * dh:(h + 1) * dh]

    def dot_nt(a, b_mat):
        return lax.dot_general(a, b_mat, (((1,), (1,)), ((), ())), preferred_element_type=F32)

    def k_of(d, h):
        return data[d][1][0, :, h * dh:(h + 1) * dh]

    def vt_of(d, h):
        return data[d][2][0, h * dh:(h + 1) * dh, :]

    def g_col(d, h):
        row = 8 * d + 4 + h
        return gcs[d][:, row:row + 1]

    def grow(block, d, h):
        row = block * N_GATE_COLS + 8 * d + 4 + h
        return data[d][4][0, row:row + 1, :]

    for g0 in range(0, len(chains), CHAIN_GROUP):
        group = chains[g0:g0 + CHAIN_GROUP]
        if emit_h:
            qk_t = [dot_nt(k_of(d, h), q_of(d, h)) for d, h in group]
            inter = [dot_nt(stb_s[d * M_HEADS + h], q_of(d, h)) for d, h in group]
            s_t = [(qk_t[i] * jnp.exp2(jnp.where(visible[d], g_col(d, h) - grow(0, d, h), neg))).astype(BF16)
                   for i, (d, h) in enumerate(group)]
            intra = [jnp.dot(jnp.concatenate([vt_of(d, h), ones_pad], axis=0), s_t[i],
                             preferred_element_type=F32) for i, (d, h) in enumerate(group)]
        upd = []
        for d, h in group:
            ws_b = grow(2, d, h).astype(BF16)
            lhs = jnp.concatenate([vt_of(d, h) * ws_b, jnp.broadcast_to(ws_b, (STATE_PAD, l))], axis=0)
            upd.append(jnp.dot(lhs, k_of(d, h), preferred_element_type=F32))
        for i, (d, h) in enumerate(group):
            if emit_h:
                m0 = grow(3, d, h)[:, 0:1]
                sa_row = jnp.exp2(m0 - grow(0, d, h))
                both = intra[i] + sa_row * inter[i]
                den = both[dh:dh + 1, :]
                h_t = both[0:dh, :] * (1.0 / jnp.maximum(jnp.abs(den), grow(1, d, h)))
                h_refs[d][0, :, h * dh:(h + 1) * dh] = h_t.astype(BF16).T
            s0 = grow(4, d, h)[:, 0:1]
            st_new = s0 * st_s[d * M_HEADS + h] + upd[i]
            st_s[d * M_HEADS + h] = st_new
            stb_s[d * M_HEADS + h] = st_new.astype(BF16)

    if emit_state:
        @pl.when(j == nj - 1)
        def _emit():
            s_out_ref[0] = st_s[...]


def _mlstm(proj, q_col, k_col, vt, gcol, grow, l, init, emit_h, emit_state):
    b, dm, n = vt.shape
    nc = n // l
    dh = dm // M_HEADS

    def feature_major(backward):
        return pl.BlockSpec((1, dm, l), lambda bi, j: (bi, 0, nc - 1 - j if backward else j))

    def token_major(width, col, backward):
        return pl.BlockSpec((1, l, width), lambda bi, j: (bi, nc - 1 - j if backward else j, col))

    in_specs, args = [], []
    for backward in (False, True):
        in_specs += [token_major(dm, q_col, backward), token_major(dm, k_col, backward), feature_major(backward),
                     token_major(GATE_LANES, 0, backward),
                     pl.BlockSpec((1, GROW_ROWS, l), lambda bi, j, bw=backward: (bi, 0, nc - 1 - j if bw else j))]
        args += [proj, proj, vt, gcol, grow]
    state_shape = (2 * M_HEADS, dh + STATE_PAD, dh)
    state_spec = pl.BlockSpec((1,) + state_shape, lambda bi, j: (bi, 0, 0, 0))
    if init is not None:
        in_specs.append(state_spec)
        args.append(init)
    out_specs, out_shape = [], []
    if emit_h:
        out_specs += [token_major(dm, 0, False), token_major(dm, 0, True)]
        out_shape += [jax.ShapeDtypeStruct((b, n, dm), BF16)] * 2
    if emit_state:
        out_specs.append(state_spec)
        out_shape.append(jax.ShapeDtypeStruct((b,) + state_shape, F32))
    return pl.pallas_call(
        functools.partial(_mlstm_kernel, init is not None, emit_h, emit_state),
        grid=(b, nc),
        in_specs=in_specs,
        out_specs=out_specs,
        out_shape=out_shape,
        scratch_shapes=[pltpu.VMEM(state_shape, F32), pltpu.VMEM(state_shape, BF16)],
        compiler_params=_params(("arbitrary", "arbitrary")),
        name="mlstm",
    )(*args)


BAND = 256
POOL_RING = 3


def _band_matrices():
    t = np.arange(BAND)
    same_row = (t[:, None] // GRID_W) == (t[None, :] // GRID_W)
    ct, cs = t[:, None] % GRID_W, t[None, :] % GRID_W
    mats = [same_row & (cs >= ct - w // 2) & (cs <= ct + w // 2 - 1) for w in POOL_WINDOWS]
    return jnp.asarray(np.stack(mats).astype(np.float32), dtype=BF16)


def _pool_tile(u_tiles, sg_ref, band_ref, wp_ref, ps_ref, y_ref, cs_ref, p, nt, prev_ok, next_ok):
    up_ref, uc_ref, un_ref = u_tiles
    t = uc_ref.shape[0]
    gdim = wp_ref.shape[1]
    rt = t // GRID_W
    shift = GRID_W.bit_length() - 1
    tok = lax.broadcasted_iota(jnp.int32, (t, 1), 0)
    row = p * rt + lax.shift_right_logical(tok, shift)
    col = tok & (GRID_W - 1)

    for g, w in enumerate(POOL_WINDOWS):
        hw = w // 2
        cols = slice(g * gdim, (g + 1) * gdim)
        band = band_ref[g]
        lo = (rt - hw) * GRID_W // BAND * BAND
        hi = -(-((2 * rt + hw - 1) * GRID_W) // BAND) * BAND
        for b0 in range(lo, hi, BAND):
            seg, off = divmod(b0, t)
            if seg == 0:
                blk = jnp.where(prev_ok, up_ref[off:off + BAND, cols], 0)
            elif seg == 1:
                blk = uc_ref[off:off + BAND, cols]
            else:
                blk = jnp.where(next_ok, un_ref[off:off + BAND, cols], 0)
            cs_ref[b0:b0 + BAND, :] = jnp.dot(band, blk, preferred_element_type=F32)
        acc = cs_ref[(rt - hw) * GRID_W:(rt - hw) * GRID_W + t, :]
        for jr in range(-hw + 1, hw):
            acc = acc + cs_ref[(rt + jr) * GRID_W:(rt + jr) * GRID_W + t, :]
        cnt_r = jnp.minimum(row + hw, nt * rt) - jnp.maximum(row - hw, 0)
        cnt_c = jnp.minimum(col + hw, GRID_W) - jnp.maximum(col - hw, 0)
        inv = 1.0 / (cnt_r * cnt_c).astype(F32)
        delta = acc * inv - uc_ref[:, cols].astype(F32)
        po = jnp.dot(delta.astype(BF16), wp_ref[g], preferred_element_type=F32)
        y_ref[0, :, cols] = (sg_ref[:, cols].astype(F32) * (po * ps_ref[:, cols])).astype(BF16)


FFN_CHUNK = 256


def _tail_kernel(x_ref, hf_ref, hb_ref, yp_ref, so_ref, sgm_ref, g1_ref, sh2_ref, sc2_ref, g2_ref,
                 mhg_ref, wo_ref, nf_ref, wi_ref, wo2_ref, nfin_ref, o_ref):
    d = x_ref.shape[2]
    dh = d // M_HEADS
    f = wo2_ref.shape[0]

    hs = hf_ref[0].astype(F32) + hb_ref[0].astype(F32)
    parts = []
    for h in range(M_HEADS):
        hh = hs[:, h * dh:(h + 1) * dh]
        parts.append(hh * _rms_scale(hh, dh))
    m_out = jnp.concatenate(parts, axis=-1) * mhg_ref[...] * so_ref[0].astype(F32)
    y = yp_ref[0].astype(F32) + sgm_ref[0].astype(F32) * m_out
    mix = jnp.dot(y.astype(BF16), wo_ref[...], preferred_element_type=F32)
    x1 = x_ref[0] + g1_ref[0] * mix

    a2 = ((x1 * _rms_scale(x1, d)) * nf_ref[...] * (1.0 + sc2_ref[0]) + sh2_ref[0]).astype(BF16)
    acc = jnp.zeros_like(x1)
    for c0 in range(0, f, FFN_CHUNK):
        c1 = min(c0 + FFN_CHUNK, f)
        gate = jnp.dot(a2, wi_ref[:, c0:c1], preferred_element_type=F32)
        up = jnp.dot(a2, wi_ref[:, f + c0:f + c1], preferred_element_type=F32)
        act = (_silu(gate) * up).astype(BF16)
        acc = acc + jnp.dot(act, wo2_ref[c0:c1, :], preferred_element_type=F32)
    x2 = x1 + g2_ref[0] * acc
    o_ref[0] = (x2 * _rms_scale(x2, d)) * nfin_ref[...]


def _tail(x, hf, hb, ypool, proj, so_col, sgm_col, mod, mh_gain, w_out, norm_ffn, w_ffn_in, w_ffn_out,
          norm_final):
    b, n, d = x.shape
    t = min(TOKEN_TILE, n)
    assert w_ffn_out.shape[0] % 256 == 0 and FFN_CHUNK % 256 == 0

    def tile(col=0):
        return pl.BlockSpec((1, t, d), lambda bi, i: (bi, i, col))

    def mod_spec(col):
        return pl.BlockSpec((1, 1, d), lambda bi, i: (bi, 0, col))

    return pl.pallas_call(
        _tail_kernel,
        grid=(b, n // t),
        in_specs=[tile(), tile(), tile(), tile(), tile(so_col), tile(sgm_col),
                  mod_spec(2), mod_spec(3), mod_spec(4), mod_spec(5),
                  _resident(mh_gain.shape), _resident(w_out.shape), _resident(norm_ffn.shape),
                  _resident(w_ffn_in.shape), _resident(w_ffn_out.shape), _resident(norm_final.shape)],
        out_specs=tile(),
        out_shape=jax.ShapeDtypeStruct((b, n, d), F32),
        compiler_params=_params(("arbitrary", "arbitrary")),
        name="tail",
    )(x, hf, hb, ypool, proj, proj, mod, mod, mod, mod, mh_gain, w_out, norm_ffn, w_ffn_in, w_ffn_out,
      norm_final)


def kernel(x, c, ctx, c_ctx, norm_mix, norm_ffn, norm_final, w_ada, b_ada, w_in, b_gates, conv_w, conv_b,
           w_pool, pool_scale, mh_gain, w_out, w_ffn_in, w_ffn_out):
    assert w_ada.shape[0] == 1, "single-layer stack only"
    b, n, d = x.shape
    assert d % (128 * M_HEADS) == 0 and w_in.shape[2] == 7 * d + N_GATE_COLS
    dh = d // M_HEADS

    pad_rows = -(b + 1) % 8
    c_rows = jnp.concatenate([c, c_ctx[None], jnp.zeros((pad_rows, d), F32)], axis=0)
    mod_all = _ada(c_rows, w_ada[0], b_ada)
    mod = mod_all[:b, None, :]
    mod_ctx = mod_all[b:b + 1, None, :]

    w_main = _cast_bf16(w_in[0], 7 * d)
    w_gate = jnp.pad(w_in[0, :, 7 * d:], ((0, 0), (0, GATE_LANES - N_GATE_COLS))).astype(BF16)
    b_gate = jnp.pad(b_gates, ((0, 0), (0, GATE_LANES - N_GATE_COLS)))
    q_scale = dh ** -0.5

    groups, gdim = w_pool.shape[1], w_pool.shape[2]
    w_pool_b = _cast_bf16(w_pool[0].reshape(groups * gdim, gdim)).reshape(groups, gdim, gdim)

    lat_kinds = (("plain", "pool_u"), ("q", "proj"), ("k", "proj"), ("plain", "tr"),
                 ("sig", "proj"), ("sig", "pool_gate"), ("sig", "proj"))
    ctx_kinds = (("skip", ""), ("q", "proj"), ("k", "proj"), ("plain", "tr"))
    proj, vt, gates, ypool = _in_proj(x, mod, (0, 1), norm_mix, w_main, w_gate, conv_w[0], conv_b, b_gate,
                                      lat_kinds, q_scale, pool=(_band_matrices(), w_pool_b, pool_scale))
    proj_c, vt_c, gates_c = _in_proj(ctx, mod_ctx, (0, 1), norm_mix, w_main, w_gate, conv_w[0],
                                     conv_b, b_gate, ctx_kinds, q_scale)

    l_lat = min(SCAN_CHUNK, n)
    l_ctx = min(SCAN_CHUNK, ctx.shape[1])
    gcol_c, grow_c, m_c = _gateprep(gates_c, l_ctx, None)
    gcol, grow, _ = _gateprep(gates, l_lat, m_c)
    (state,) = _mlstm(proj_c, 0, 1, vt_c, gcol_c, grow_c, l_ctx, None, False, True)
    h_dirs = _mlstm(proj, 0, 1, vt, gcol, grow, l_lat, state, True, False)

    return _tail(x, h_dirs[0], h_dirs[1], ypool, proj, 2, 3, mod, mh_gain, _cast_bf16(w_out[0]), norm_ffn,
                 _cast_bf16(w_ffn_in[0]), _cast_bf16(w_ffn_out[0]), norm_final[None])
```

```python
import functools
import math

import numpy as np
import jax
import jax.numpy as jnp
from jax import lax
from jax.experimental import pallas as pl
from jax.experimental.pallas import tpu as pltpu

F32 = jnp.float32
BF16 = jnp.bfloat16

EPS = 1e-6
LOG2E = math.log2(math.e)
GRID_W = 64
POOL_WINDOWS = (2, 4, 8, 16)
M_HEADS = 4
CONV_W = 3
N_GATE_COLS = 4 * M_HEADS
GATE_LANES = 128
SCAN_CHUNK = 256
TOKEN_TILE = 512
HALO_ROWS = 16
VMEM_LIMIT = 60 * 1024 * 1024


def _resident(shape):
    nd = len(shape)
    return pl.BlockSpec(shape, lambda *_: (0,) * nd, pipeline_mode=pl.Buffered(1))


def _params(sem):
    return pltpu.CompilerParams(dimension_semantics=sem, vmem_limit_bytes=VMEM_LIMIT)


def _sigmoid(x):
    return 1.0 / (1.0 + jnp.exp(-x))


def _silu(x):
    return x * _sigmoid(x)


def _rms_scale(xf, d):
    return lax.rsqrt(jnp.sum(xf * xf, axis=-1, keepdims=True) * (1.0 / d) + EPS)


def _ada_kernel(c_ref, w_ref, b_ref, o_ref):
    s = _silu(c_ref[...])
    o_ref[...] = jnp.dot(s.astype(BF16), w_ref[...].astype(BF16), preferred_element_type=F32) + b_ref[...]


def _ada(c_rows, w_ada, b_ada):
    r, d = c_rows.shape
    n = w_ada.shape[1]
    return pl.pallas_call(
        _ada_kernel,
        grid=(n // d,),
        in_specs=[pl.BlockSpec((r, d), lambda j: (0, 0)),
                  pl.BlockSpec((d, d), lambda j: (0, j)),
                  pl.BlockSpec((1, d), lambda j: (0, j))],
        out_specs=pl.BlockSpec((r, d), lambda j: (0, j)),
        out_shape=jax.ShapeDtypeStruct((r, n), F32),
        compiler_params=_params(("arbitrary",)),
        name="ada",
    )(c_rows, w_ada, b_ada)


def _cast_kernel(w_ref, o_ref):
    o_ref[...] = w_ref[...].astype(BF16)


def _cast_bf16(w, ncols=None):
    k, n = w.shape
    ncols = n if ncols is None else ncols
    block = max(bw for bw in range(128, min(ncols, 1024) + 1, 128) if ncols % bw == 0)
    return pl.pallas_call(
        _cast_kernel,
        grid=(ncols // block,),
        in_specs=[pl.BlockSpec((k, block), lambda j: (0, j))],
        out_specs=pl.BlockSpec((k, block), lambda j: (0, j)),
        out_shape=jax.ShapeDtypeStruct((k, ncols), BF16),
        compiler_params=_params(("arbitrary",)),
        name="wcast",
    )(w)


def _in_proj_kernel(kinds, q_scale, with_pool, xp_ref, xc_ref, xn_ref, sh_ref, sc_ref, gain_ref, w_ref, wg_ref,
                    cw_ref, cb_ref, bg_ref, *rest):
    rest = list(rest)
    if with_pool:
        band_ref, wp_ref, ps_ref = rest[:3]
        rest = rest[3:]
    n_tr = sum(dest == "tr" for _, dest in kinds)
    proj_ref = rest[0]
    tr_refs = rest[1:1 + n_tr]
    gates_ref = rest[1 + n_tr]
    rest = rest[2 + n_tr:]
    if with_pool:
        ypool_ref, a_ref, qk_ref, u_ring, sg_ring, cs_ref = rest
    else:
        a_ref, qk_ref = rest
    i = pl.program_id(1)
    nt = pl.num_programs(1) - (1 if with_pool else 0)
    t = xc_ref.shape[1]
    d = xc_ref.shape[2]

    def project():
        gain = gain_ref[...]
        mult = 1.0 + sc_ref[0]
        shift = sh_ref[0]

        def norm_mod(xf):
            return (xf * _rms_scale(xf, d)) * gain * mult + shift

        prev = jnp.where(i > 0, norm_mod(xp_ref[0]), 0.0)
        nxt = jnp.where(i < nt - 1, norm_mod(xn_ref[0]), 0.0)
        zeros8 = jnp.zeros((HALO_ROWS - 8, d), F32)
        a_ref[0:HALO_ROWS, :] = jnp.concatenate([zeros8, prev], axis=0).astype(BF16)
        a_ref[HALO_ROWS:HALO_ROWS + t, :] = norm_mod(xc_ref[0]).astype(BF16)
        a_ref[HALO_ROWS + t:2 * HALO_ROWS + t, :] = jnp.concatenate([nxt, zeros8], axis=0).astype(BF16)

        out_col = 0
        out_tr = 0
        for g, (kind, dest) in enumerate(kinds):
            cols = slice(g * d, (g + 1) * d)
            if kind == "skip":
                continue
            if kind in ("q", "k"):
                qk_ref[...] = jnp.dot(a_ref[...], w_ref[:, cols], preferred_element_type=F32)
                c0 = 0 if kind == "q" else d
                cw = cw_ref[:, c0:c0 + d]
                y = (cb_ref[:, c0:c0 + d]
                     + qk_ref[HALO_ROWS - 1:HALO_ROWS - 1 + t, :] * cw[0:1]
                     + qk_ref[HALO_ROWS:HALO_ROWS + t, :] * cw[1:2]
                     + qk_ref[HALO_ROWS + 1:HALO_ROWS + 1 + t, :] * cw[2:3])
                y = _silu(y)
                if kind == "q":
                    y = y * q_scale
            else:
                y = jnp.dot(a_ref[HALO_ROWS:HALO_ROWS + t, :], w_ref[:, cols], preferred_element_type=F32)
                if kind == "sig":
                    y = _sigmoid(y)
            if dest == "tr":
                y_t = y.T.astype(BF16)
                l = tr_refs[out_tr].shape[3]
                for c in range(t // l):
                    tr_refs[out_tr][0, c] = y_t[:, c * l:(c + 1) * l]
                out_tr += 1
            elif dest == "pool_u":
                u_ring[lax.rem(i, POOL_RING)] = y.astype(BF16)
            elif dest == "pool_gate":
                sg_ring[lax.rem(i, 2)] = y.astype(BF16)
            else:
                proj_ref[0, :, out_col * d:(out_col + 1) * d] = y.astype(BF16)
                out_col += 1

        gates = jnp.dot(a_ref[HALO_ROWS:HALO_ROWS + t, :], wg_ref[...], preferred_element_type=F32) + bg_ref[...]
        gates_ref[0] = gates.T[0:N_GATE_COLS, :]

    if not with_pool:
        project()
        return

    def pool():
        p = i - 1
        u_tiles = [u_ring.at[lax.rem(p + POOL_RING - 1, POOL_RING)], u_ring.at[lax.rem(p, POOL_RING)],
                   u_ring.at[lax.rem(p + 1, POOL_RING)]]
        _pool_tile(u_tiles, sg_ring.at[lax.rem(p, 2)], band_ref, wp_ref, ps_ref, ypool_ref, cs_ref,
                   p, nt, p >= 1, p < nt - 1)

    pl.when(i < nt)(project)
    pl.when(i >= 1)(pool)


def _in_proj(x, mod, mod_cols, gain, w, wg, conv_w, conv_b, bg, kinds, q_scale, l, pool=None):
    b, n, d = x.shape
    t = min(TOKEN_TILE, n)
    nt = n // t
    per_batch = mod.shape[0] > 1
    sh_col, sc_col = mod_cols
    with_pool = pool is not None

    def tile_of(i):
        return jnp.minimum(i, nt - 1)

    def mod_map(col):
        return lambda bi, i: (bi if per_batch else 0, 0, col)

    t8 = t // 8
    last8 = n // 8 - 1
    in_specs = [
        pl.BlockSpec((1, 8, d), lambda bi, i: (bi, jnp.maximum(tile_of(i) * t8 - 1, 0), 0)),
        pl.BlockSpec((1, t, d), lambda bi, i: (bi, tile_of(i), 0)),
        pl.BlockSpec((1, 8, d), lambda bi, i: (bi, jnp.minimum((tile_of(i) + 1) * t8, last8), 0)),
        pl.BlockSpec((1, 1, d), mod_map(sh_col)),
        pl.BlockSpec((1, 1, d), mod_map(sc_col)),
        _resident(gain.shape), _resident((d, len(kinds) * d)), _resident(wg.shape),
        _resident(conv_w.shape), _resident(conv_b.shape), _resident(bg.shape),
    ]
    args = [x, x, x, mod, mod, gain, w, wg, conv_w, conv_b, bg]
    n_tr = sum(dest == "tr" for _, dest in kinds)
    ncol = sum(kind != "skip" and dest == "proj" for kind, dest in kinds) * d
    out_specs = ([pl.BlockSpec((1, t, ncol), lambda bi, i: (bi, tile_of(i), 0))]
                 + [pl.BlockSpec((1, t // l, d, l), lambda bi, i: (bi, tile_of(i), 0, 0))] * n_tr
                 + [pl.BlockSpec((1, N_GATE_COLS, t), lambda bi, i: (bi, 0, tile_of(i)))])
    out_shape = ([jax.ShapeDtypeStruct((b, n, ncol), BF16)]
                 + [jax.ShapeDtypeStruct((b, n // l, d, l), BF16)] * n_tr
                 + [jax.ShapeDtypeStruct((b, N_GATE_COLS, n), F32)])
    scratch = [pltpu.VMEM((t + 2 * HALO_ROWS, d), BF16), pltpu.VMEM((t + 2 * HALO_ROWS, d), F32)]
    if with_pool:
        assert t == TOKEN_TILE and n % t == 0 and t % BAND == 0 and BAND % GRID_W == 0
        assert GRID_W & (GRID_W - 1) == 0 and t // GRID_W >= max(POOL_WINDOWS) // 2
        in_specs += [_resident(a.shape) for a in pool]
        args += list(pool)
        out_specs.append(pl.BlockSpec((1, t, d), lambda bi, i: (bi, jnp.maximum(i - 1, 0), 0)))
        out_shape.append(jax.ShapeDtypeStruct((b, n, d), BF16))
        scratch += [pltpu.VMEM((POOL_RING, t, d), BF16), pltpu.VMEM((2, t, d), BF16),
                    pltpu.VMEM((3 * t, d // len(POOL_WINDOWS)), F32)]
    return pl.pallas_call(
        functools.partial(_in_proj_kernel, kinds, q_scale, with_pool),
        grid=(b, nt + 1 if with_pool else nt),
        in_specs=in_specs,
        out_specs=out_specs,
        out_shape=out_shape,
        scratch_shapes=scratch,
        compiler_params=_params(("arbitrary", "arbitrary")),
        name="in_proj",
    )(*args)


GROW_BLOCKS = 5
GROW_ROWS = GROW_BLOCKS * N_GATE_COLS


def _gate_direction(x, m_init, l, backward):
    n = x.shape[1]
    nc = n // l
    pos = lax.broadcasted_iota(jnp.int32, x.shape, 1) & (l - 1)
    neg = jnp.float32(-jnp.inf)

    def shifted(v, s, fill):
        if backward:
            return jnp.where(pos < l - s, pltpu.roll(v, n - s, axis=1), fill)
        return jnp.where(pos >= s, pltpu.roll(v, s, axis=1), fill)

    csum = jnp.minimum(x, 0.0) - jnp.log1p(jnp.exp(-jnp.abs(x)))
    s = 1
    while s < l:
        csum = csum + shifted(csum, s, 0.0)
        s *= 2
    g = pltpu.roll(x, 4, axis=0) - csum
    gmax = g
    s = 1
    while s < l:
        gmax = jnp.maximum(gmax, shifted(gmax, s, neg))
        s *= 2

    m0_parts, ml_parts = [None] * nc, [None] * nc
    m0 = m_init
    for step in range(nc):
        ci = nc - 1 - step if backward else step
        last = ci * l if backward else ci * l + l - 1
        m_last = jnp.maximum(m0, gmax[:, last:last + 1])
        m0_parts[ci] = jnp.broadcast_to(m0, (8, l))
        ml_parts[ci] = jnp.broadcast_to(m_last, (8, l))
        m0 = csum[:, last:last + 1] + m_last
    m0_arr = jnp.concatenate(m0_parts, axis=1)
    ml_arr = jnp.concatenate(ml_parts, axis=1)
    big_m = jnp.maximum(m0_arr, gmax)
    f_rows = lax.broadcasted_iota(jnp.int32, x.shape, 0) >= 4
    rows = (big_m * LOG2E, jnp.exp(-(csum + big_m)), jnp.exp(g - ml_arr), m0_arr * LOG2E,
            jnp.exp(m0_arr - ml_arr))
    rows = tuple(jnp.where(f_rows, v, 0.0) for v in rows)
    return rows, jnp.where(f_rows, g * LOG2E, 0.0), m0


def _gateprep_kernel(l, has_init, *refs):
    refs = list(refs)
    gates_ref = refs.pop(0)
    m_init_ref = refs.pop(0) if has_init else None
    gcol_ref, grow_ref, m_out_ref = refs
    n = gates_ref.shape[2]
    col_blocks = []
    for d in range(2):
        rows8 = slice(8 * d, 8 * d + 8)
        m_init = m_init_ref[0, rows8, 0:1] if has_init else jnp.zeros((8, 1), F32)
        rows, g2, m_fin = _gate_direction(gates_ref[0, rows8, :], m_init, l, backward=d == 1)
        for k, v in enumerate(rows):
            for c in range(n // l):
                grow_ref[0, c, k * N_GATE_COLS + 8 * d:k * N_GATE_COLS + 8 * d + 8, :] = v[:, c * l:(c + 1) * l]
        col_blocks.append(g2)
        m_out_ref[0, rows8, :] = jnp.broadcast_to(m_fin, (8, GATE_LANES))
    col_src = jnp.concatenate(col_blocks + [jnp.zeros((GATE_LANES - N_GATE_COLS, n), F32)], axis=0)
    gcol_ref[0] = col_src.T


def _gateprep(gates, l, m_init):
    b, _, n = gates.shape
    assert l & (l - 1) == 0 and n % l == 0
    in_specs = [pl.BlockSpec((1, N_GATE_COLS, n), lambda bi: (bi, 0, 0))]
    args = [gates]
    if m_init is not None:
        in_specs.append(pl.BlockSpec((1, N_GATE_COLS, GATE_LANES), lambda bi: (bi, 0, 0)))
        args.append(m_init)
    return pl.pallas_call(
        functools.partial(_gateprep_kernel, l, m_init is not None),
        grid=(b,),
        in_specs=in_specs,
        out_specs=[pl.BlockSpec((1, n, GATE_LANES), lambda bi: (bi, 0, 0)),
                   pl.BlockSpec((1, n // l, GROW_ROWS, l), lambda bi: (bi, 0, 0, 0)),
                   pl.BlockSpec((1, N_GATE_COLS, GATE_LANES), lambda bi: (bi, 0, 0))],
        out_shape=[jax.ShapeDtypeStruct((b, n, GATE_LANES), F32),
                   jax.ShapeDtypeStruct((b, n // l, GROW_ROWS, l), F32),
                   jax.ShapeDtypeStruct((b, N_GATE_COLS, GATE_LANES), F32)],
        compiler_params=_params(("arbitrary",)),
        name="gateprep",
    )(*args)


STATE_PAD = 16
CHAIN_GROUP = 2


def _mlstm_kernel(has_init, emit_h, emit_state, *refs):
    refs = list(refs)
    data = [refs[0:5], refs[5:10]]
    pos = 10
    if has_init:
        s_init_ref = refs[pos]
        pos += 1
    if emit_h:
        h_refs = refs[pos:pos + 2]
        pos += 2
    if emit_state:
        s_out_ref = refs[pos]
        pos += 1
    st_s, stb_s = refs[pos:pos + 2]

    j = pl.program_id(1)
    nj = pl.num_programs(1)
    l = data[0][1].shape[1]
    dh = data[0][1].shape[2] // M_HEADS

    @pl.when(j == 0)
    def _init():
        if has_init:
            st_s[...] = s_init_ref[0]
            stb_s[...] = s_init_ref[0].astype(BF16)
        else:
            st_s[...] = jnp.zeros_like(st_s)
            stb_s[...] = jnp.zeros_like(stb_s)

    r = lax.broadcasted_iota(jnp.int32, (l, l), 0)
    c = lax.broadcasted_iota(jnp.int32, (l, l), 1)
    neg = jnp.float32(-jnp.inf)
    ones_pad = jnp.ones((STATE_PAD, l), BF16)

    chains = [(d, h) for d in range(2) for h in range(M_HEADS)]
    visible = [r <= c, r >= c]
    gcs = [data[d][3][0] for d in range(2)]

    def q_of(d, h):
        return data[d][0][0, :, h * dh:(h + 1) * dh]

    def dot_nt(a, b_mat):
        return lax.dot_general(a, b_mat, (((1,), (1,)), ((), ())), preferred_element_type=F32)

    def k_of(d, h):
        return data[d][1][0, :, h * dh:(h + 1) * dh]

    def vt_of(d, h):
        return data[d][2][0, 0, h * dh:(h + 1) * dh, :]

    def g_col(d, h):
        row = 8 * d + 4 + h
        return gcs[d][:, row:row + 1]

    def grow(block, d, h):
        row = block * N_GATE_COLS + 8 * d + 4 + h
        return data[d][4][0, 0, row:row + 1, :]

    for g0 in range(0, len(chains), CHAIN_GROUP):
        group = chains[g0:g0 + CHAIN_GROUP]
        if emit_h:
            qk_t = [dot_nt(k_of(d, h), q_of(d, h)) for d, h in group]
            inter = [dot_nt(stb_s[d * M_HEADS + h], q_of(d, h)) for d, h in group]
            s_t = [(qk_t[i] * jnp.exp2(jnp.where(visible[d], g_col(d, h) - grow(0, d, h), neg))).astype(BF16)
                   for i, (d, h) in enumerate(group)]
            intra = [jnp.dot(jnp.concatenate([vt_of(d, h), ones_pad], axis=0), s_t[i],
                             preferred_element_type=F32) for i, (d, h) in enumerate(group)]
        upd = []
        for d, h in group:
            ws_b = grow(2, d, h).astype(BF16)
            lhs = jnp.concatenate([vt_of(d, h) * ws_b, jnp.broadcast_to(ws_b, (STATE_PAD, l))], axis=0)
            upd.append(jnp.dot(lhs, k_of(d, h), preferred_element_type=F32))
        for i, (d, h) in enumerate(group):
            if emit_h:
                m0 = grow(3, d, h)[:, 0:1]
                sa_row = jnp.exp2(m0 - grow(0, d, h))
                both = intra[i] + sa_row * inter[i]
                den = both[dh:dh + 1, :]
                h_t = both[0:dh, :] * (1.0 / jnp.maximum(jnp.abs(den), grow(1, d, h)))
                h_refs[d][0, :, h * dh:(h + 1) * dh] = h_t.astype(BF16).T
            s0 = grow(4, d, h)[:, 0:1]
            st_new = s0 * st_s[d * M_HEADS + h] + upd[i]
            st_s[d * M_HEADS + h] = st_new
            stb_s[d * M_HEADS + h] = st_new.astype(BF16)

    if emit_state:
        @pl.when(j == nj - 1)
        def _emit():
            s_out_ref[0] = st_s[...]


def _mlstm(proj, q_col, k_col, vt, gcol, grow, l, init, emit_h, emit_state):
    b, _, dm, _ = vt.shape
    n = proj.shape[1]
    nc = n // l
    dh = dm // M_HEADS

    def chunk_slab(rows, backward):
        return pl.BlockSpec((1, 1, rows, l), lambda bi, j: (bi, nc - 1 - j if backward else j, 0, 0))

    def token_major(width, col, backward):
        return pl.BlockSpec((1, l, width), lambda bi, j: (bi, nc - 1 - j if backward else j, col))

    in_specs, args = [], []
    for backward in (False, True):
        in_specs += [token_major(dm, q_col, backward), token_major(dm, k_col, backward), chunk_slab(dm, backward),
                     token_major(GATE_LANES, 0, backward), chunk_slab(GROW_ROWS, backward)]
        args += [proj, proj, vt, gcol, grow]
    state_shape = (2 * M_HEADS, dh + STATE_PAD, dh)
    state_spec = pl.BlockSpec((1,) + state_shape, lambda bi, j: (bi, 0, 0, 0))
    if init is not None:
        in_specs.append(state_spec)
        args.append(init)
    out_specs, out_shape = [], []
    if emit_h:
        out_specs += [token_major(dm, 0, False), token_major(dm, 0, True)]
        out_shape += [jax.ShapeDtypeStruct((b, n, dm), BF16)] * 2
    if emit_state:
        out_specs.append(state_spec)
        out_shape.append(jax.ShapeDtypeStruct((b,) + state_shape, F32))
    return pl.pallas_call(
        functools.partial(_mlstm_kernel, init is not None, emit_h, emit_state),
        grid=(b, nc),
        in_specs=in_specs,
        out_specs=out_specs,
        out_shape=out_shape,
        scratch_shapes=[pltpu.VMEM(state_shape, F32), pltpu.VMEM(state_shape, BF16)],
        compiler_params=_params(("arbitrary", "arbitrary")),
        name="mlstm",
    )(*args)


BAND = 256
POOL_RING = 3


def _band_matrices():
    t = np.arange(BAND)
    same_row = (t[:, None] // GRID_W) == (t[None, :] // GRID_W)
    ct, cs = t[:, None] % GRID_W, t[None, :] % GRID_W
    mats = [same_row & (cs >= ct - w // 2) & (cs <= ct + w // 2 - 1) for w in POOL_WINDOWS]
    return jnp.asarray(np.stack(mats).astype(np.float32), dtype=BF16)


def _pool_tile(u_tiles, sg_ref, band_ref, wp_ref, ps_ref, y_ref, cs_ref, p, nt, prev_ok, next_ok):
    up_ref, uc_ref, un_ref = u_tiles
    t = uc_ref.shape[0]
    gdim = wp_ref.shape[1]
    rt = t // GRID_W
    shift = GRID_W.bit_length() - 1
    tok = lax.broadcasted_iota(jnp.int32, (t, 1), 0)
    row = p * rt + lax.shift_right_logical(tok, shift)
    col = tok & (GRID_W - 1)

    for g, w in enumerate(POOL_WINDOWS):
        hw = w // 2
        cols = slice(g * gdim, (g + 1) * gdim)
        band = band_ref[g]
        lo = (rt - hw) * GRID_W // BAND * BAND
        hi = -(-((2 * rt + hw - 1) * GRID_W) // BAND) * BAND
        for b0 in range(lo, hi, BAND):
            seg, off = divmod(b0, t)
            if seg == 0:
                blk = jnp.where(prev_ok, up_ref[off:off + BAND, cols], 0)
            elif seg == 1:
                blk = uc_ref[off:off + BAND, cols]
            else:
                blk = jnp.where(next_ok, un_ref[off:off + BAND, cols], 0)
            cs_ref[b0:b0 + BAND, :] = jnp.dot(band, blk, preferred_element_type=F32)
        acc = cs_ref[(rt - hw) * GRID_W:(rt - hw) * GRID_W + t, :]
        for jr in range(-hw + 1, hw):
            acc = acc + cs_ref[(rt + jr) * GRID_W:(rt + jr) * GRID_W + t, :]
        cnt_r = jnp.minimum(row + hw, nt * rt) - jnp.maximum(row - hw, 0)
        cnt_c = jnp.minimum(col + hw, GRID_W) - jnp.maximum(col - hw, 0)
        inv = 1.0 / (cnt_r * cnt_c).astype(F32)
        delta = acc * inv - uc_ref[:, cols].astype(F32)
        po = jnp.dot(delta.astype(BF16), wp_ref[g], preferred_element_type=F32)
        y_ref[0, :, cols] = (sg_ref[:, cols].astype(F32) * (po * ps_ref[:, cols])).astype(BF16)


FFN_CHUNK = 256


def _tail_kernel(x_ref, hf_ref, hb_ref, yp_ref, so_ref, sgm_ref, g1_ref, sh2_ref, sc2_ref, g2_ref,
                 mhg_ref, wo_ref, nf_ref, wi_ref, wo2_ref, nfin_ref, o_ref):
    d = x_ref.shape[2]
    dh = d // M_HEADS
    f = wo2_ref.shape[0]

    hs = hf_ref[0].astype(F32) + hb_ref[0].astype(F32)
    parts = []
    for h in range(M_HEADS):
        hh = hs[:, h * dh:(h + 1) * dh]
        parts.append(hh * _rms_scale(hh, dh))
    m_out = jnp.concatenate(parts, axis=-1) * mhg_ref[...] * so_ref[0].astype(F32)
    y = yp_ref[0].astype(F32) + sgm_ref[0].astype(F32) * m_out
    mix = jnp.dot(y.astype(BF16), wo_ref[...], preferred_element_type=F32)
    x1 = x_ref[0] + g1_ref[0] * mix

    a2 = ((x1 * _rms_scale(x1, d)) * nf_ref[...] * (1.0 + sc2_ref[0]) + sh2_ref[0]).astype(BF16)
    acc = jnp.zeros_like(x1)
    for c0 in range(0, f, FFN_CHUNK):
        c1 = min(c0 + FFN_CHUNK, f)
        gate = jnp.dot(a2, wi_ref[:, c0:c1], preferred_element_type=F32)
        up = jnp.dot(a2, wi_ref[:, f + c0:f + c1], preferred_element_type=F32)
        act = (_silu(gate) * up).astype(BF16)
        acc = acc + jnp.dot(act, wo2_ref[c0:c1, :], preferred_element_type=F32)
    x2 = x1 + g2_ref[0] * acc
    o_ref[0] = (x2 * _rms_scale(x2, d)) * nfin_ref[...]


def _tail(x, hf, hb, ypool, proj, so_col, sgm_col, mod, mh_gain, w_out, norm_ffn, w_ffn_in, w_ffn_out,
          norm_final):
    b, n, d = x.shape
    t = min(TOKEN_TILE, n)
    assert w_ffn_out.shape[0] % 256 == 0 and FFN_CHUNK % 256 == 0

    def tile(col=0):
        return pl.BlockSpec((1, t, d), lambda bi, i: (bi, i, col))

    def mod_spec(col):
        return pl.BlockSpec((1, 1, d), lambda bi, i: (bi, 0, col))

    return pl.pallas_call(
        _tail_kernel,
        grid=(b, n // t),
        in_specs=[tile(), tile(), tile(), tile(), tile(so_col), tile(sgm_col),
                  mod_spec(2), mod_spec(3), mod_spec(4), mod_spec(5),
                  _resident(mh_gain.shape), _resident(w_out.shape), _resident(norm_ffn.shape),
                  _resident(w_ffn_in.shape), _resident(w_ffn_out.shape), _resident(norm_final.shape)],
        out_specs=tile(),
        out_shape=jax.ShapeDtypeStruct((b, n, d), F32),
        compiler_params=_params(("arbitrary", "arbitrary")),
        name="tail",
    )(x, hf, hb, ypool, proj, proj, mod, mod, mod, mod, mh_gain, w_out, norm_ffn, w_ffn_in, w_ffn_out,
      norm_final)


def kernel(x, c, ctx, c_ctx, norm_mix, norm_ffn, norm_final, w_ada, b_ada, w_in, b_gates, conv_w, conv_b,
           w_pool, pool_scale, mh_gain, w_out, w_ffn_in, w_ffn_out):
    assert w_ada.shape[0] == 1, "single-layer stack only"
    b, n, d = x.shape
    assert d % (128 * M_HEADS) == 0 and w_in.shape[2] == 7 * d + N_GATE_COLS
    dh = d // M_HEADS

    pad_rows = -(b + 1) % 8
    c_rows = jnp.concatenate([c, c_ctx[None], jnp.zeros((pad_rows, d), F32)], axis=0)
    mod_all = _ada(c_rows, w_ada[0], b_ada)
    mod = mod_all[:b, None, :]
    mod_ctx = mod_all[b:b + 1, None, :]

    w_main = _cast_bf16(w_in[0], 7 * d)
    w_gate = jnp.pad(w_in[0, :, 7 * d:], ((0, 0), (0, GATE_LANES - N_GATE_COLS))).astype(BF16)
    b_gate = jnp.pad(b_gates, ((0, 0), (0, GATE_LANES - N_GATE_COLS)))
    q_scale = dh ** -0.5

    groups, gdim = w_pool.shape[1], w_pool.shape[2]
    w_pool_b = _cast_bf16(w_pool[0].reshape(groups * gdim, gdim)).reshape(groups, gdim, gdim)

    lat_kinds = (("plain", "pool_u"), ("q", "proj"), ("k", "proj"), ("plain", "tr"),
                 ("sig", "proj"), ("sig", "pool_gate"), ("sig", "proj"))
    ctx_kinds = (("skip", ""), ("q", "proj"), ("k", "proj"), ("plain", "tr"))
    l_lat = min(SCAN_CHUNK, n)
    l_ctx = min(SCAN_CHUNK, ctx.shape[1])
    proj, vt, gates, ypool = _in_proj(x, mod, (0, 1), norm_mix, w_main, w_gate, conv_w[0], conv_b, b_gate,
                                      lat_kinds, q_scale, l_lat, pool=(_band_matrices(), w_pool_b, pool_scale))
    proj_c, vt_c, gates_c = _in_proj(ctx, mod_ctx, (0, 1), norm_mix, w_main, w_gate, conv_w[0],
                                     conv_b, b_gate, ctx_kinds, q_scale, l_ctx)

    gcol_c, grow_c, m_c = _gateprep(gates_c, l_ctx, None)
    gcol, grow, _ = _gateprep(gates, l_lat, m_c)
    (state,) = _mlstm(proj_c, 0, 1, vt_c, gcol_c, grow_c, l_ctx, None, False, True)
    h_dirs = _mlstm(proj, 0, 1, vt, gcol, grow, l_lat, state, True, False)

    return _tail(x, h_dirs[0], h_dirs[1], ypool, proj, 2, 3, mod, mh_gain, _cast_bf16(w_out[0]), norm_ffn,
                 _cast_bf16(w_ffn_in[0]), _cast_bf16(w_ffn_out[0]), norm_final[None])
```

```python
import functools
import math

import numpy as np
import jax
import jax.numpy as jnp
from jax import lax
from jax.experimental import pallas as pl
from jax.experimental.pallas import tpu as pltpu

F32 = jnp.float32
BF16 = jnp.bfloat16

EPS = 1e-6
LOG2E = math.log2(math.e)
GRID_W = 64
POOL_WINDOWS = (2, 4, 8, 16)
M_HEADS = 4
CONV_W = 3
N_GATE_COLS = 4 * M_HEADS
GATE_LANES = 128
SCAN_CHUNK = 256
TOKEN_TILE = 512
HALO_ROWS = 16
VMEM_LIMIT = 60 * 1024 * 1024


def _resident(shape):
    nd = len(shape)
    return pl.BlockSpec(shape, lambda *_: (0,) * nd, pipeline_mode=pl.Buffered(1))


def _params(sem):
    return pltpu.CompilerParams(dimension_semantics=sem, vmem_limit_bytes=VMEM_LIMIT)


def _sigmoid(x):
    return 1.0 / (1.0 + jnp.exp(-x))


def _silu(x):
    return x * _sigmoid(x)


def _rms_scale(xf, d):
    return lax.rsqrt(jnp.sum(xf * xf, axis=-1, keepdims=True) * (1.0 / d) + EPS)


def _ada_kernel(c_ref, w_ref, b_ref, o_ref):
    s = _silu(c_ref[...])
    o_ref[...] = jnp.dot(s.astype(BF16), w_ref[...].astype(BF16), preferred_element_type=F32) + b_ref[...]


def _ada(c_rows, w_ada, b_ada):
    r, d = c_rows.shape
    n = w_ada.shape[1]
    return pl.pallas_call(
        _ada_kernel,
        grid=(n // d,),
        in_specs=[pl.BlockSpec((r, d), lambda j: (0, 0)),
                  pl.BlockSpec((d, d), lambda j: (0, j)),
                  pl.BlockSpec((1, d), lambda j: (0, j))],
        out_specs=pl.BlockSpec((r, d), lambda j: (0, j)),
        out_shape=jax.ShapeDtypeStruct((r, n), F32),
        compiler_params=_params(("arbitrary",)),
        name="ada",
    )(c_rows, w_ada, b_ada)


def _cast_kernel(w_ref, o_ref):
    o_ref[...] = w_ref[...].astype(BF16)


def _cast_bf16(w, ncols=None):
    k, n = w.shape
    ncols = n if ncols is None else ncols
    block = max(bw for bw in range(128, min(ncols, 1024) + 1, 128) if ncols % bw == 0)
    return pl.pallas_call(
        _cast_kernel,
        grid=(ncols // block,),
        in_specs=[pl.BlockSpec((k, block), lambda j: (0, j))],
        out_specs=pl.BlockSpec((k, block), lambda j: (0, j)),
        out_shape=jax.ShapeDtypeStruct((k, ncols), BF16),
        compiler_params=_params(("arbitrary",)),
        name="wcast",
    )(w)


def _cast_t_kernel(w_ref, o_ref):
    o_ref[...] = w_ref[...].T.astype(BF16)


def _cast_bf16_from_transposed(wt, ncols, block=1024):
    _, k = wt.shape
    assert ncols % block == 0
    return pl.pallas_call(
        _cast_t_kernel,
        grid=(ncols // block,),
        in_specs=[pl.BlockSpec((block, k), lambda j: (j, 0))],
        out_specs=pl.BlockSpec((k, block), lambda j: (0, j)),
        out_shape=jax.ShapeDtypeStruct((k, ncols), BF16),
        compiler_params=_params(("arbitrary",)),
        name="wcast_t",
    )(wt)


def _in_proj_kernel(kinds, q_scale, with_pool, xp_ref, xc_ref, xn_ref, sh_ref, sc_ref, gain_ref, w_ref, wg_ref,
                    cw_ref, cb_ref, bg_ref, *rest):
    rest = list(rest)
    if with_pool:
        band_ref, wp_ref, ps_ref = rest[:3]
        rest = rest[3:]
    n_tr = sum(dest == "tr" for _, dest in kinds)
    proj_ref = rest[0]
    tr_refs = rest[1:1 + n_tr]
    gates_ref = rest[1 + n_tr]
    rest = rest[2 + n_tr:]
    if with_pool:
        ypool_ref, a_ref, qk_ref, u_ring, sg_ring, cs_ref = rest
    else:
        a_ref, qk_ref = rest
    i = pl.program_id(1)
    nt = pl.num_programs(1) - (1 if with_pool else 0)
    t = xc_ref.shape[1]
    d = xc_ref.shape[2]

    def project():
        gain = gain_ref[...]
        mult = 1.0 + sc_ref[0]
        shift = sh_ref[0]

        def norm_mod(xf):
            return (xf * _rms_scale(xf, d)) * gain * mult + shift

        prev = jnp.where(i > 0, norm_mod(xp_ref[0]), 0.0)
        nxt = jnp.where(i < nt - 1, norm_mod(xn_ref[0]), 0.0)
        zeros8 = jnp.zeros((HALO_ROWS - 8, d), F32)
        a_ref[0:HALO_ROWS, :] = jnp.concatenate([zeros8, prev], axis=0).astype(BF16)
        a_ref[HALO_ROWS:HALO_ROWS + t, :] = norm_mod(xc_ref[0]).astype(BF16)
        a_ref[HALO_ROWS + t:2 * HALO_ROWS + t, :] = jnp.concatenate([nxt, zeros8], axis=0).astype(BF16)

        out_col = 0
        out_tr = 0
        for g, (kind, dest) in enumerate(kinds):
            cols = slice(g * d, (g + 1) * d)
            if kind == "skip":
                continue
            if kind in ("q", "k"):
                qk_ref[...] = jnp.dot(a_ref[...], w_ref[:, cols], preferred_element_type=F32)
                c0 = 0 if kind == "q" else d
                cw = cw_ref[:, c0:c0 + d]
                y = (cb_ref[:, c0:c0 + d]
                     + qk_ref[HALO_ROWS - 1:HALO_ROWS - 1 + t, :] * cw[0:1]
                     + qk_ref[HALO_ROWS:HALO_ROWS + t, :] * cw[1:2]
                     + qk_ref[HALO_ROWS + 1:HALO_ROWS + 1 + t, :] * cw[2:3])
                y = _silu(y)
                if kind == "q":
                    y = y * q_scale
            else:
                y = jnp.dot(a_ref[HALO_ROWS:HALO_ROWS + t, :], w_ref[:, cols], preferred_element_type=F32)
                if kind == "sig":
                    y = _sigmoid(y)
            if dest == "tr":
                y_t = y.T.astype(BF16)
                l = tr_refs[out_tr].shape[3]
                for c in range(t // l):
                    tr_refs[out_tr][0, c] = y_t[:, c * l:(c + 1) * l]
                out_tr += 1
            elif dest == "pool_u":
                u_ring[lax.rem(i, POOL_RING)] = y.astype(BF16)
            elif dest == "pool_gate":
                sg_ring[lax.rem(i, 2)] = y.astype(BF16)
            else:
                proj_ref[0, :, out_col * d:(out_col + 1) * d] = y.astype(BF16)
                out_col += 1

        gates = jnp.dot(a_ref[HALO_ROWS:HALO_ROWS + t, :], wg_ref[...], preferred_element_type=F32) + bg_ref[...]
        gates_ref[0] = gates.T[0:N_GATE_COLS, :]

    if not with_pool:
        project()
        return

    def pool():
        p = i - 1
        u_tiles = [u_ring.at[lax.rem(p + POOL_RING - 1, POOL_RING)], u_ring.at[lax.rem(p, POOL_RING)],
                   u_ring.at[lax.rem(p + 1, POOL_RING)]]
        _pool_tile(u_tiles, sg_ring.at[lax.rem(p, 2)], band_ref, wp_ref, ps_ref, ypool_ref, cs_ref,
                   p, nt, p >= 1, p < nt - 1)

    pl.when(i < nt)(project)
    pl.when(i >= 1)(pool)


def _in_proj(x, mod, mod_cols, gain, w, wg, conv_w, conv_b, bg, kinds, q_scale, l, pool=None):
    b, n, d = x.shape
    t = min(TOKEN_TILE, n)
    nt = n // t
    per_batch = mod.shape[0] > 1
    sh_col, sc_col = mod_cols
    with_pool = pool is not None

    def tile_of(i):
        return jnp.minimum(i, nt - 1)

    def mod_map(col):
        return lambda bi, i: (bi if per_batch else 0, 0, col)

    t8 = t // 8
    last8 = n // 8 - 1
    in_specs = [
        pl.BlockSpec((1, 8, d), lambda bi, i: (bi, jnp.maximum(tile_of(i) * t8 - 1, 0), 0)),
        pl.BlockSpec((1, t, d), lambda bi, i: (bi, tile_of(i), 0)),
        pl.BlockSpec((1, 8, d), lambda bi, i: (bi, jnp.minimum((tile_of(i) + 1) * t8, last8), 0)),
        pl.BlockSpec((1, 1, d), mod_map(sh_col)),
        pl.BlockSpec((1, 1, d), mod_map(sc_col)),
        _resident(gain.shape), _resident((d, len(kinds) * d)), _resident(wg.shape),
        _resident(conv_w.shape), _resident(conv_b.shape), _resident(bg.shape),
    ]
    args = [x, x, x, mod, mod, gain, w, wg, conv_w, conv_b, bg]
    n_tr = sum(dest == "tr" for _, dest in kinds)
    ncol = sum(kind != "skip" and dest == "proj" for kind, dest in kinds) * d
    out_specs = ([pl.BlockSpec((1, t, ncol), lambda bi, i: (bi, tile_of(i), 0))]
                 + [pl.BlockSpec((1, t // l, d, l), lambda bi, i: (bi, tile_of(i), 0, 0))] * n_tr
                 + [pl.BlockSpec((1, N_GATE_COLS, t), lambda bi, i: (bi, 0, tile_of(i)))])
    out_shape = ([jax.ShapeDtypeStruct((b, n, ncol), BF16)]
                 + [jax.ShapeDtypeStruct((b, n // l, d, l), BF16)] * n_tr
                 + [jax.ShapeDtypeStruct((b, N_GATE_COLS, n), F32)])
    scratch = [pltpu.VMEM((t + 2 * HALO_ROWS, d), BF16), pltpu.VMEM((t + 2 * HALO_ROWS, d), F32)]
    if with_pool:
        assert t == TOKEN_TILE and n % t == 0 and t % BAND == 0 and BAND % GRID_W == 0
        assert GRID_W & (GRID_W - 1) == 0 and t // GRID_W >= max(POOL_WINDOWS) // 2
        in_specs += [_resident(a.shape) for a in pool]
        args += list(pool)
        out_specs.append(pl.BlockSpec((1, t, d), lambda bi, i: (bi, jnp.maximum(i - 1, 0), 0)))
        out_shape.append(jax.ShapeDtypeStruct((b, n, d), BF16))
        scratch += [pltpu.VMEM((POOL_RING, t, d), BF16), pltpu.VMEM((2, t, d), BF16),
                    pltpu.VMEM((3 * t, d // len(POOL_WINDOWS)), F32)]
    return pl.pallas_call(
        functools.partial(_in_proj_kernel, kinds, q_scale, with_pool),
        grid=(b, nt + 1 if with_pool else nt),
        in_specs=in_specs,
        out_specs=out_specs,
        out_shape=out_shape,
        scratch_shapes=scratch,
        compiler_params=_params(("arbitrary", "arbitrary")),
        name="in_proj",
    )(*args)


GROW_BLOCKS = 5
GROW_ROWS = GROW_BLOCKS * N_GATE_COLS


def _gate_direction(x, m_init, l, backward):
    n = x.shape[1]
    nc = n // l
    pos = lax.broadcasted_iota(jnp.int32, x.shape, 1) & (l - 1)
    neg = jnp.float32(-jnp.inf)

    def shifted(v, s, fill):
        if backward:
            return jnp.where(pos < l - s, pltpu.roll(v, n - s, axis=1), fill)
        return jnp.where(pos >= s, pltpu.roll(v, s, axis=1), fill)

    csum = jnp.minimum(x, 0.0) - jnp.log1p(jnp.exp(-jnp.abs(x)))
    s = 1
    while s < l:
        csum = csum + shifted(csum, s, 0.0)
        s *= 2
    g = pltpu.roll(x, 4, axis=0) - csum
    gmax = g
    s = 1
    while s < l:
        gmax = jnp.maximum(gmax, shifted(gmax, s, neg))
        s *= 2

    m0_parts, ml_parts = [None] * nc, [None] * nc
    m0 = m_init
    for step in range(nc):
        ci = nc - 1 - step if backward else step
        last = ci * l if backward else ci * l + l - 1
        m_last = jnp.maximum(m0, gmax[:, last:last + 1])
        m0_parts[ci] = jnp.broadcast_to(m0, (8, l))
        ml_parts[ci] = jnp.broadcast_to(m_last, (8, l))
        m0 = csum[:, last:last + 1] + m_last
    m0_arr = jnp.concatenate(m0_parts, axis=1)
    ml_arr = jnp.concatenate(ml_parts, axis=1)
    big_m = jnp.maximum(m0_arr, gmax)
    f_rows = lax.broadcasted_iota(jnp.int32, x.shape, 0) >= 4
    rows = (big_m * LOG2E, jnp.exp(-(csum + big_m)), jnp.exp(g - ml_arr), m0_arr * LOG2E,
            jnp.exp(m0_arr - ml_arr))
    rows = tuple(jnp.where(f_rows, v, 0.0) for v in rows)
    return rows, jnp.where(f_rows, g * LOG2E, 0.0), m0


def _gateprep_kernel(l, has_init, *refs):
    refs = list(refs)
    gates_ref = refs.pop(0)
    m_init_ref = refs.pop(0) if has_init else None
    gcol_ref, grow_ref, m_out_ref = refs
    n = gates_ref.shape[2]
    col_blocks = []
    for d in range(2):
        rows8 = slice(8 * d, 8 * d + 8)
        m_init = m_init_ref[0, rows8, 0:1] if has_init else jnp.zeros((8, 1), F32)
        rows, g2, m_fin = _gate_direction(gates_ref[0, rows8, :], m_init, l, backward=d == 1)
        for k, v in enumerate(rows):
            for c in range(n // l):
                grow_ref[0, c, k * N_GATE_COLS + 8 * d:k * N_GATE_COLS + 8 * d + 8, :] = v[:, c * l:(c + 1) * l]
        col_blocks.append(g2)
        m_out_ref[0, rows8, :] = jnp.broadcast_to(m_fin, (8, GATE_LANES))
    col_src = jnp.concatenate(col_blocks + [jnp.zeros((GATE_LANES - N_GATE_COLS, n), F32)], axis=0)
    gcol_ref[0] = col_src.T


def _gateprep(gates, l, m_init):
    b, _, n = gates.shape
    assert l & (l - 1) == 0 and n % l == 0
    in_specs = [pl.BlockSpec((1, N_GATE_COLS, n), lambda bi: (bi, 0, 0))]
    args = [gates]
    if m_init is not None:
        in_specs.append(pl.BlockSpec((1, N_GATE_COLS, GATE_LANES), lambda bi: (bi, 0, 0)))
        args.append(m_init)
    return pl.pallas_call(
        functools.partial(_gateprep_kernel, l, m_init is not None),
        grid=(b,),
        in_specs=in_specs,
        out_specs=[pl.BlockSpec((1, n, GATE_LANES), lambda bi: (bi, 0, 0)),
                   pl.BlockSpec((1, n // l, GROW_ROWS, l), lambda bi: (bi, 0, 0, 0)),
                   pl.BlockSpec((1, N_GATE_COLS, GATE_LANES), lambda bi: (bi, 0, 0))],
        out_shape=[jax.ShapeDtypeStruct((b, n, GATE_LANES), F32),
                   jax.ShapeDtypeStruct((b, n // l, GROW_ROWS, l), F32),
                   jax.ShapeDtypeStruct((b, N_GATE_COLS, GATE_LANES), F32)],
        compiler_params=_params(("arbitrary",)),
        name="gateprep",
    )(*args)


STATE_PAD = 16
CHAIN_GROUP = 2
SCAN_STEP_CHUNKS = 4


def _mlstm_kernel(has_init, emit_h, emit_state, *refs):
    refs = list(refs)
    data = [refs[0:5], refs[5:10]]
    pos = 10
    if has_init:
        s_init_ref = refs[pos]
        pos += 1
    if emit_h:
        h_refs = refs[pos:pos + 2]
        pos += 2
    if emit_state:
        s_out_ref = refs[pos]
        pos += 1
    st_s, stb_s = refs[pos:pos + 2]

    j = pl.program_id(1)
    nj = pl.num_programs(1)
    cps, l = data[0][2].shape[1], data[0][2].shape[3]
    dh = data[0][1].shape[2] // M_HEADS

    @pl.when(j == 0)
    def _init():
        if has_init:
            st_s[...] = s_init_ref[0]
            stb_s[...] = s_init_ref[0].astype(BF16)
        else:
            st_s[...] = jnp.zeros_like(st_s)
            stb_s[...] = jnp.zeros_like(stb_s)

    r = lax.broadcasted_iota(jnp.int32, (l, l), 0)
    c = lax.broadcasted_iota(jnp.int32, (l, l), 1)
    neg = jnp.float32(-jnp.inf)
    ones_pad = jnp.ones((STATE_PAD, l), BF16)

    chains = [(s if d == 0 else cps - 1 - s, d, h) for s in range(cps) for d in range(2) for h in range(M_HEADS)]
    visible = [r <= c, r >= c]

    def dot_nt(a, b_mat):
        return lax.dot_general(a, b_mat, (((1,), (1,)), ((), ())), preferred_element_type=F32)

    def q_of(c, d, h):
        return data[d][0][0, c * l:(c + 1) * l, h * dh:(h + 1) * dh]

    def k_of(c, d, h):
        return data[d][1][0, c * l:(c + 1) * l, h * dh:(h + 1) * dh]

    def vt_of(c, d, h):
        return data[d][2][0, c, h * dh:(h + 1) * dh, :]

    def g_col(c, d, h):
        row = 8 * d + 4 + h
        return data[d][3][0, c * l:(c + 1) * l, row:row + 1]

    def grow(block, c, d, h):
        row = block * N_GATE_COLS + 8 * d + 4 + h
        return data[d][4][0, c, row:row + 1, :]

    for g0 in range(0, len(chains), CHAIN_GROUP):
        group = chains[g0:g0 + CHAIN_GROUP]
        if emit_h:
            qk_t = [dot_nt(k_of(c, d, h), q_of(c, d, h)) for c, d, h in group]
            inter = [dot_nt(stb_s[d * M_HEADS + h], q_of(c, d, h)) for c, d, h in group]
            s_t = [(qk_t[i] * jnp.exp2(jnp.where(visible[d], g_col(c, d, h) - grow(0, c, d, h), neg))
                    ).astype(BF16) for i, (c, d, h) in enumerate(group)]
            intra = [jnp.dot(jnp.concatenate([vt_of(c, d, h), ones_pad], axis=0), s_t[i],
                             preferred_element_type=F32) for i, (c, d, h) in enumerate(group)]
        upd = []
        for c, d, h in group:
            ws_b = grow(2, c, d, h).astype(BF16)
            lhs = jnp.concatenate([vt_of(c, d, h) * ws_b, jnp.broadcast_to(ws_b, (STATE_PAD, l))], axis=0)
            upd.append(jnp.dot(lhs, k_of(c, d, h), preferred_element_type=F32))
        for i, (c, d, h) in enumerate(group):
            if emit_h:
                m0 = grow(3, c, d, h)[:, 0:1]
                sa_row = jnp.exp2(m0 - grow(0, c, d, h))
                both = intra[i] + sa_row * inter[i]
                den = both[dh:dh + 1, :]
                h_t = both[0:dh, :] * (1.0 / jnp.maximum(jnp.abs(den), grow(1, c, d, h)))
                h_refs[d][0, c * l:(c + 1) * l, h * dh:(h + 1) * dh] = h_t.astype(BF16).T
            s0 = grow(4, c, d, h)[:, 0:1]
            st_new = s0 * st_s[d * M_HEADS + h] + upd[i]
            st_s[d * M_HEADS + h] = st_new
            stb_s[d * M_HEADS + h] = st_new.astype(BF16)

    if emit_state:
        @pl.when(j == nj - 1)
        def _emit():
            s_out_ref[0] = st_s[...]


def _mlstm(proj, q_col, k_col, vt, gcol, grow, l, init, emit_h, emit_state):
    b, _, dm, _ = vt.shape
    n = proj.shape[1]
    nc = n // l
    cps = SCAN_STEP_CHUNKS if nc % SCAN_STEP_CHUNKS == 0 else 1
    nb = nc // cps
    dh = dm // M_HEADS

    def chunk_slab(rows, backward):
        return pl.BlockSpec((1, cps, rows, l), lambda bi, j: (bi, nb - 1 - j if backward else j, 0, 0))

    def token_major(width, col, backward):
        return pl.BlockSpec((1, cps * l, width), lambda bi, j: (bi, nb - 1 - j if backward else j, col))

    in_specs, args = [], []
    for backward in (False, True):
        in_specs += [token_major(dm, q_col, backward), token_major(dm, k_col, backward), chunk_slab(dm, backward),
                     token_major(GATE_LANES, 0, backward), chunk_slab(GROW_ROWS, backward)]
        args += [proj, proj, vt, gcol, grow]
    state_shape = (2 * M_HEADS, dh + STATE_PAD, dh)
    state_spec = pl.BlockSpec((1,) + state_shape, lambda bi, j: (bi, 0, 0, 0))
    if init is not None:
        in_specs.append(state_spec)
        args.append(init)
    out_specs, out_shape = [], []
    if emit_h:
        out_specs += [token_major(dm, 0, False), token_major(dm, 0, True)]
        out_shape += [jax.ShapeDtypeStruct((b, n, dm), BF16)] * 2
    if emit_state:
        out_specs.append(state_spec)
        out_shape.append(jax.ShapeDtypeStruct((b,) + state_shape, F32))
    return pl.pallas_call(
        functools.partial(_mlstm_kernel, init is not None, emit_h, emit_state),
        grid=(b, nb),
        in_specs=in_specs,
        out_specs=out_specs,
        out_shape=out_shape,
        scratch_shapes=[pltpu.VMEM(state_shape, F32), pltpu.VMEM(state_shape, BF16)],
        compiler_params=_params(("arbitrary", "arbitrary")),
        name="mlstm",
    )(*args)


BAND = 256
POOL_RING = 3


def _band_matrices():
    t = np.arange(BAND)
    same_row = (t[:, None] // GRID_W) == (t[None, :] // GRID_W)
    ct, cs = t[:, None] % GRID_W, t[None, :] % GRID_W
    mats = [same_row & (cs >= ct - w // 2) & (cs <= ct + w // 2 - 1) for w in POOL_WINDOWS]
    return jnp.asarray(np.stack(mats).astype(np.float32), dtype=BF16)


def _pool_tile(u_tiles, sg_ref, band_ref, wp_ref, ps_ref, y_ref, cs_ref, p, nt, prev_ok, next_ok):
    up_ref, uc_ref, un_ref = u_tiles
    t = uc_ref.shape[0]
    gdim = wp_ref.shape[1]
    rt = t // GRID_W
    shift = GRID_W.bit_length() - 1
    tok = lax.broadcasted_iota(jnp.int32, (t, 1), 0)
    row = p * rt + lax.shift_right_logical(tok, shift)
    col = tok & (GRID_W - 1)

    for g, w in enumerate(POOL_WINDOWS):
        hw = w // 2
        cols = slice(g * gdim, (g + 1) * gdim)
        band = band_ref[g]
        lo = (rt - hw) * GRID_W // BAND * BAND
        hi = -(-((2 * rt + hw - 1) * GRID_W) // BAND) * BAND
        for b0 in range(lo, hi, BAND):
            seg, off = divmod(b0, t)
            if seg == 0:
                blk = jnp.where(prev_ok, up_ref[off:off + BAND, cols], 0)
            elif seg == 1:
                blk = uc_ref[off:off + BAND, cols]
            else:
                blk = jnp.where(next_ok, un_ref[off:off + BAND, cols], 0)
            cs_ref[b0:b0 + BAND, :] = jnp.dot(band, blk, preferred_element_type=F32)
        acc = cs_ref[(rt - hw) * GRID_W:(rt - hw) * GRID_W + t, :]
        for jr in range(-hw + 1, hw):
            acc = acc + cs_ref[(rt + jr) * GRID_W:(rt + jr) * GRID_W + t, :]
        cnt_r = jnp.minimum(row + hw, nt * rt) - jnp.maximum(row - hw, 0)
        cnt_c = jnp.minimum(col + hw, GRID_W) - jnp.maximum(col - hw, 0)
        inv = 1.0 / (cnt_r * cnt_c).astype(F32)
        delta = acc * inv - uc_ref[:, cols].astype(F32)
        po = jnp.dot(delta.astype(BF16), wp_ref[g], preferred_element_type=F32)
        y_ref[0, :, cols] = (sg_ref[:, cols].astype(F32) * (po * ps_ref[:, cols])).astype(BF16)


FFN_CHUNK = 256


def _tail_kernel(x_ref, hf_ref, hb_ref, yp_ref, so_ref, sgm_ref, g1_ref, sh2_ref, sc2_ref, g2_ref,
                 mhg_ref, wo_ref, nf_ref, wi_ref, wo2_ref, nfin_ref, o_ref):
    d = x_ref.shape[2]
    dh = d // M_HEADS
    f = wo2_ref.shape[0]

    hs = hf_ref[0].astype(F32) + hb_ref[0].astype(F32)
    parts = []
    for h in range(M_HEADS):
        hh = hs[:, h * dh:(h + 1) * dh]
        parts.append(hh * _rms_scale(hh, dh))
    m_out = jnp.concatenate(parts, axis=-1) * mhg_ref[...] * so_ref[0].astype(F32)
    y = yp_ref[0].astype(F32) + sgm_ref[0].astype(F32) * m_out
    mix = jnp.dot(y.astype(BF16), wo_ref[...], preferred_element_type=F32)
    x1 = x_ref[0] + g1_ref[0] * mix

    a2 = ((x1 * _rms_scale(x1, d)) * nf_ref[...] * (1.0 + sc2_ref[0]) + sh2_ref[0]).astype(BF16)
    acc = jnp.zeros_like(x1)
    for c0 in range(0, f, FFN_CHUNK):
        c1 = min(c0 + FFN_CHUNK, f)
        gate = jnp.dot(a2, wi_ref[:, c0:c1], preferred_element_type=F32)
        up = jnp.dot(a2, wi_ref[:, f + c0:f + c1], preferred_element_type=F32)
        act = (_silu(gate) * up).astype(BF16)
        acc = acc + jnp.dot(act, wo2_ref[c0:c1, :], preferred_element_type=F32)
    x2 = x1 + g2_ref[0] * acc
    o_ref[0] = (x2 * _rms_scale(x2, d)) * nfin_ref[...]


def _tail(x, hf, hb, ypool, proj, so_col, sgm_col, mod, mh_gain, w_out, norm_ffn, w_ffn_in, w_ffn_out,
          norm_final):
    b, n, d = x.shape
    t = min(TOKEN_TILE, n)
    assert w_ffn_out.shape[0] % 256 == 0 and FFN_CHUNK % 256 == 0

    def tile(col=0):
        return pl.BlockSpec((1, t, d), lambda bi, i: (bi, i, col))

    def mod_spec(col):
        return pl.BlockSpec((1, 1, d), lambda bi, i: (bi, 0, col))

    return pl.pallas_call(
        _tail_kernel,
        grid=(b, n // t),
        in_specs=[tile(), tile(), tile(), tile(), tile(so_col), tile(sgm_col),
                  mod_spec(2), mod_spec(3), mod_spec(4), mod_spec(5),
                  _resident(mh_gain.shape), _resident(w_out.shape), _resident(norm_ffn.shape),
                  _resident(w_ffn_in.shape), _resident(w_ffn_out.shape), _resident(norm_final.shape)],
        out_specs=tile(),
        out_shape=jax.ShapeDtypeStruct((b, n, d), F32),
        compiler_params=_params(("arbitrary", "arbitrary")),
        name="tail",
    )(x, hf, hb, ypool, proj, proj, mod, mod, mod, mod, mh_gain, w_out, norm_ffn, w_ffn_in, w_ffn_out,
      norm_final)


def kernel(x, c, ctx, c_ctx, norm_mix, norm_ffn, norm_final, w_ada, b_ada, w_in, b_gates, conv_w, conv_b,
           w_pool, pool_scale, mh_gain, w_out, w_ffn_in, w_ffn_out):
    assert w_ada.shape[0] == 1, "single-layer stack only"
    b, n, d = x.shape
    assert d % (128 * M_HEADS) == 0 and w_in.shape[2] == 7 * d + N_GATE_COLS
    dh = d // M_HEADS

    pad_rows = -(b + 1) % 8
    c_rows = jnp.concatenate([c, c_ctx[None], jnp.zeros((pad_rows, d), F32)], axis=0)
    mod_all = _ada(c_rows, w_ada[0], b_ada)
    mod = mod_all[:b, None, :]
    mod_ctx = mod_all[b:b + 1, None, :]

    w_main = _cast_bf16_from_transposed(jnp.swapaxes(w_in[0], 0, 1), 7 * d)
    w_gate = jnp.pad(w_in[0, :, 7 * d:], ((0, 0), (0, GATE_LANES - N_GATE_COLS))).astype(BF16)
    b_gate = jnp.pad(b_gates, ((0, 0), (0, GATE_LANES - N_GATE_COLS)))
    q_scale = dh ** -0.5

    groups, gdim = w_pool.shape[1], w_pool.shape[2]
    w_pool_b = _cast_bf16(w_pool[0].reshape(groups * gdim, gdim)).reshape(groups, gdim, gdim)

    lat_kinds = (("plain", "pool_u"), ("q", "proj"), ("k", "proj"), ("plain", "tr"),
                 ("sig", "proj"), ("sig", "pool_gate"), ("sig", "proj"))
    ctx_kinds = (("skip", ""), ("q", "proj"), ("k", "proj"), ("plain", "tr"))
    l_lat = min(SCAN_CHUNK, n)
    l_ctx = min(SCAN_CHUNK, ctx.shape[1])
    proj, vt, gates, ypool = _in_proj(x, mod, (0, 1), norm_mix, w_main, w_gate, conv_w[0], conv_b, b_gate,
                                      lat_kinds, q_scale, l_lat, pool=(_band_matrices(), w_pool_b, pool_scale))
    proj_c, vt_c, gates_c = _in_proj(ctx, mod_ctx, (0, 1), norm_mix, w_main, w_gate, conv_w[0],
                                     conv_b, b_gate, ctx_kinds, q_scale, l_ctx)

    gcol_c, grow_c, m_c = _gateprep(gates_c, l_ctx, None)
    gcol, grow, _ = _gateprep(gates, l_lat, m_c)
    (state,) = _mlstm(proj_c, 0, 1, vt_c, gcol_c, grow_c, l_ctx, None, False, True)
    h_dirs = _mlstm(proj, 0, 1, vt, gcol, grow, l_lat, state, True, False)

    return _tail(x, h_dirs[0], h_dirs[1], ypool, proj, 2, 3, mod, mh_gain, _cast_bf16(w_out[0]), norm_ffn,
                 _cast_bf16(w_ffn_in[0]), _cast_bf16(w_ffn_out[0]), norm_final[None])
```

```python
import functools
import math

import numpy as np
import jax
import jax.numpy as jnp
from jax import lax
from jax.experimental import pallas as pl
from jax.experimental.pallas import tpu as pltpu

F32 = jnp.float32
BF16 = jnp.bfloat16

EPS = 1e-6
LOG2E = math.log2(math.e)
GRID_W = 64
POOL_WINDOWS = (2, 4, 8, 16)
M_HEADS = 4
CONV_W = 3
N_GATE_COLS = 4 * M_HEADS
GATE_LANES = 128
SCAN_CHUNK = 256
TOKEN_TILE = 512
HALO_ROWS = 16
VMEM_LIMIT = 60 * 1024 * 1024


def _resident(shape):
    nd = len(shape)
    return pl.BlockSpec(shape, lambda *_: (0,) * nd, pipeline_mode=pl.Buffered(1))


def _params(sem):
    return pltpu.CompilerParams(dimension_semantics=sem, vmem_limit_bytes=VMEM_LIMIT)


def _sigmoid(x):
    return 1.0 / (1.0 + jnp.exp(-x))


def _silu(x):
    return x * _sigmoid(x)


def _rms_scale(xf, d):
    return lax.rsqrt(jnp.sum(xf * xf, axis=-1, keepdims=True) * (1.0 / d) + EPS)


def _ada_kernel(c_ref, w_ref, b_ref, o_ref):
    s = _silu(c_ref[...])
    o_ref[...] = jnp.dot(s.astype(BF16), w_ref[...].astype(BF16), preferred_element_type=F32) + b_ref[...]


def _ada(c_rows, w_ada, b_ada):
    r, d = c_rows.shape
    n = w_ada.shape[1]
    return pl.pallas_call(
        _ada_kernel,
        grid=(n // d,),
        in_specs=[pl.BlockSpec((r, d), lambda j: (0, 0)),
                  pl.BlockSpec((d, d), lambda j: (0, j)),
                  pl.BlockSpec((1, d), lambda j: (0, j))],
        out_specs=pl.BlockSpec((r, d), lambda j: (0, j)),
        out_shape=jax.ShapeDtypeStruct((r, n), F32),
        compiler_params=_params(("arbitrary",)),
        name="ada",
    )(c_rows, w_ada, b_ada)


def _cast_kernel(w_ref, o_ref):
    o_ref[...] = w_ref[...].astype(BF16)


def _cast_bf16(w, ncols=None):
    k, n = w.shape
    ncols = n if ncols is None else ncols
    block = max(bw for bw in range(128, min(ncols, 1024) + 1, 128) if ncols % bw == 0)
    return pl.pallas_call(
        _cast_kernel,
        grid=(ncols // block,),
        in_specs=[pl.BlockSpec((k, block), lambda j: (0, j))],
        out_specs=pl.BlockSpec((k, block), lambda j: (0, j)),
        out_shape=jax.ShapeDtypeStruct((k, ncols), BF16),
        compiler_params=_params(("arbitrary",)),
        name="wcast",
    )(w)


def _cast_t_kernel(w_ref, o_ref):
    o_ref[...] = w_ref[...].T.astype(BF16)


def _cast_bf16_from_transposed(wt, ncols, block=1024):
    _, k = wt.shape
    assert ncols % block == 0
    return pl.pallas_call(
        _cast_t_kernel,
        grid=(ncols // block,),
        in_specs=[pl.BlockSpec((block, k), lambda j: (j, 0))],
        out_specs=pl.BlockSpec((k, block), lambda j: (0, j)),
        out_shape=jax.ShapeDtypeStruct((k, ncols), BF16),
        compiler_params=_params(("arbitrary",)),
        name="wcast_t",
    )(wt)


def _in_proj_kernel(kinds, q_scale, with_pool, xp_ref, xc_ref, xn_ref, sh_ref, sc_ref, gain_ref, w_ref, wg_ref,
                    cw_ref, cb_ref, bg_ref, *rest):
    rest = list(rest)
    if with_pool:
        band_ref, wp_ref, ps_ref = rest[:3]
        rest = rest[3:]
    n_tr = sum(dest == "tr" for _, dest in kinds)
    proj_ref = rest[0]
    tr_refs = rest[1:1 + n_tr]
    gates_ref = rest[1 + n_tr]
    rest = rest[2 + n_tr:]
    if with_pool:
        ypool_ref, a_ref, qk_ref, u_ring, sg_ring, cs_ref = rest
    else:
        a_ref, qk_ref = rest
    i = pl.program_id(1)
    nt = pl.num_programs(1) - (1 if with_pool else 0)
    t = xc_ref.shape[1]
    d = xc_ref.shape[2]

    def project():
        gain = gain_ref[...]
        mult = 1.0 + sc_ref[0]
        shift = sh_ref[0]

        def norm_mod(xf):
            return (xf * _rms_scale(xf, d)) * gain * mult + shift

        prev = jnp.where(i > 0, norm_mod(xp_ref[0]), 0.0)
        nxt = jnp.where(i < nt - 1, norm_mod(xn_ref[0]), 0.0)
        zeros8 = jnp.zeros((HALO_ROWS - 8, d), F32)
        a_ref[0:HALO_ROWS, :] = jnp.concatenate([zeros8, prev], axis=0).astype(BF16)
        a_ref[HALO_ROWS:HALO_ROWS + t, :] = norm_mod(xc_ref[0]).astype(BF16)
        a_ref[HALO_ROWS + t:2 * HALO_ROWS + t, :] = jnp.concatenate([nxt, zeros8], axis=0).astype(BF16)

        out_col = 0
        out_tr = 0
        for g, (kind, dest) in enumerate(kinds):
            cols = slice(g * d, (g + 1) * d)
            if kind == "skip":
                continue
            if kind in ("q", "k"):
                pre = jnp.dot(a_ref[...], w_ref[:, cols], preferred_element_type=F32)
                rows = t + 2 * HALO_ROWS
                c0 = 0 if kind == "q" else d
                cw = cw_ref[:, c0:c0 + d]
                y = (cb_ref[:, c0:c0 + d]
                     + pltpu.roll(pre, 1, axis=0)[HALO_ROWS:HALO_ROWS + t, :] * cw[0:1]
                     + pre[HALO_ROWS:HALO_ROWS + t, :] * cw[1:2]
                     + pltpu.roll(pre, rows - 1, axis=0)[HALO_ROWS:HALO_ROWS + t, :] * cw[2:3])
                y = _silu(y)
                if kind == "q":
                    y = y * q_scale
            else:
                y = jnp.dot(a_ref[HALO_ROWS:HALO_ROWS + t, :], w_ref[:, cols], preferred_element_type=F32)
                if kind == "sig":
                    y = _sigmoid(y)
            if dest == "tr":
                y_t = y.T.astype(BF16)
                l = tr_refs[out_tr].shape[3]
                for c in range(t // l):
                    tr_refs[out_tr][0, c] = y_t[:, c * l:(c + 1) * l]
                out_tr += 1
            elif dest == "pool_u":
                u_ring[lax.rem(i, POOL_RING)] = y.astype(BF16)
            elif dest == "pool_gate":
                sg_ring[lax.rem(i, 2)] = y.astype(BF16)
            else:
                proj_ref[0, :, out_col * d:(out_col + 1) * d] = y.astype(BF16)
                out_col += 1

        gates = jnp.dot(a_ref[HALO_ROWS:HALO_ROWS + t, :], wg_ref[...], preferred_element_type=F32) + bg_ref[...]
        gates_ref[0] = gates.T[0:N_GATE_COLS, :]

    if not with_pool:
        project()
        return

    def pool():
        p = i - 1
        u_tiles = [u_ring.at[lax.rem(p + POOL_RING - 1, POOL_RING)], u_ring.at[lax.rem(p, POOL_RING)],
                   u_ring.at[lax.rem(p + 1, POOL_RING)]]
        _pool_tile(u_tiles, sg_ring.at[lax.rem(p, 2)], band_ref, wp_ref, ps_ref, ypool_ref, cs_ref,
                   p, nt, p >= 1, p < nt - 1)

    pl.when(i < nt)(project)
    pl.when(i >= 1)(pool)


def _in_proj(x, mod, mod_cols, gain, w, wg, conv_w, conv_b, bg, kinds, q_scale, l, pool=None):
    b, n, d = x.shape
    t = min(TOKEN_TILE, n)
    nt = n // t
    per_batch = mod.shape[0] > 1
    sh_col, sc_col = mod_cols
    with_pool = pool is not None

    def tile_of(i):
        return jnp.minimum(i, nt - 1)

    def mod_map(col):
        return lambda bi, i: (bi if per_batch else 0, 0, col)

    t8 = t // 8
    last8 = n // 8 - 1
    in_specs = [
        pl.BlockSpec((1, 8, d), lambda bi, i: (bi, jnp.maximum(tile_of(i) * t8 - 1, 0), 0)),
        pl.BlockSpec((1, t, d), lambda bi, i: (bi, tile_of(i), 0)),
        pl.BlockSpec((1, 8, d), lambda bi, i: (bi, jnp.minimum((tile_of(i) + 1) * t8, last8), 0)),
        pl.BlockSpec((1, 1, d), mod_map(sh_col)),
        pl.BlockSpec((1, 1, d), mod_map(sc_col)),
        _resident(gain.shape), _resident((d, len(kinds) * d)), _resident(wg.shape),
        _resident(conv_w.shape), _resident(conv_b.shape), _resident(bg.shape),
    ]
    args = [x, x, x, mod, mod, gain, w, wg, conv_w, conv_b, bg]
    n_tr = sum(dest == "tr" for _, dest in kinds)
    ncol = sum(kind != "skip" and dest == "proj" for kind, dest in kinds) * d
    out_specs = ([pl.BlockSpec((1, t, ncol), lambda bi, i: (bi, tile_of(i), 0))]
                 + [pl.BlockSpec((1, t // l, d, l), lambda bi, i: (bi, tile_of(i), 0, 0))] * n_tr
                 + [pl.BlockSpec((1, N_GATE_COLS, t), lambda bi, i: (bi, 0, tile_of(i)))])
    out_shape = ([jax.ShapeDtypeStruct((b, n, ncol), BF16)]
                 + [jax.ShapeDtypeStruct((b, n // l, d, l), BF16)] * n_tr
                 + [jax.ShapeDtypeStruct((b, N_GATE_COLS, n), F32)])
    scratch = [pltpu.VMEM((t + 2 * HALO_ROWS, d), BF16), pltpu.VMEM((t + 2 * HALO_ROWS, d), F32)]
    if with_pool:
        assert t == TOKEN_TILE and n % t == 0 and t % BAND == 0 and BAND % GRID_W == 0
        assert GRID_W & (GRID_W - 1) == 0 and t // GRID_W >= max(POOL_WINDOWS) // 2
        in_specs += [_resident(a.shape) for a in pool]
        args += list(pool)
        out_specs.append(pl.BlockSpec((1, t, d), lambda bi, i: (bi, jnp.maximum(i - 1, 0), 0)))
        out_shape.append(jax.ShapeDtypeStruct((b, n, d), BF16))
        scratch += [pltpu.VMEM((POOL_RING, t, d), BF16), pltpu.VMEM((2, t, d), BF16),
                    pltpu.VMEM((3 * t, d // len(POOL_WINDOWS)), F32)]
    return pl.pallas_call(
        functools.partial(_in_proj_kernel, kinds, q_scale, with_pool),
        grid=(b, nt + 1 if with_pool else nt),
        in_specs=in_specs,
        out_specs=out_specs,
        out_shape=out_shape,
        scratch_shapes=scratch,
        compiler_params=_params(("arbitrary", "arbitrary")),
        name="in_proj",
    )(*args)


GROW_BLOCKS = 5
GROW_ROWS = GROW_BLOCKS * N_GATE_COLS


def _gate_direction(x, m_init, l, backward):
    n = x.shape[1]
    nc = n // l
    pos = lax.broadcasted_iota(jnp.int32, x.shape, 1) & (l - 1)
    neg = jnp.float32(-jnp.inf)

    def shifted(v, s, fill):
        if backward:
            return jnp.where(pos < l - s, pltpu.roll(v, n - s, axis=1), fill)
        return jnp.where(pos >= s, pltpu.roll(v, s, axis=1), fill)

    csum = jnp.minimum(x, 0.0) - jnp.log1p(jnp.exp(-jnp.abs(x)))
    s = 1
    while s < l:
        csum = csum + shifted(csum, s, 0.0)
        s *= 2
    g = pltpu.roll(x, 4, axis=0) - csum
    gmax = g
    s = 1
    while s < l:
        gmax = jnp.maximum(gmax, shifted(gmax, s, neg))
        s *= 2

    m0_parts, ml_parts = [None] * nc, [None] * nc
    m0 = m_init
    for step in range(nc):
        ci = nc - 1 - step if backward else step
        last = ci * l if backward else ci * l + l - 1
        m_last = jnp.maximum(m0, gmax[:, last:last + 1])
        m0_parts[ci] = jnp.broadcast_to(m0, (8, l))
        ml_parts[ci] = jnp.broadcast_to(m_last, (8, l))
        m0 = csum[:, last:last + 1] + m_last
    m0_arr = jnp.concatenate(m0_parts, axis=1)
    ml_arr = jnp.concatenate(ml_parts, axis=1)
    big_m = jnp.maximum(m0_arr, gmax)
    f_rows = lax.broadcasted_iota(jnp.int32, x.shape, 0) >= 4
    rows = (big_m * LOG2E, jnp.exp(-(csum + big_m)), jnp.exp(g - ml_arr), m0_arr * LOG2E,
            jnp.exp(m0_arr - ml_arr))
    rows = tuple(jnp.where(f_rows, v, 0.0) for v in rows)
    return rows, jnp.where(f_rows, g * LOG2E, 0.0), m0


def _gateprep_kernel(l, has_init, *refs):
    refs = list(refs)
    gates_ref = refs.pop(0)
    m_init_ref = refs.pop(0) if has_init else None
    gcol_ref, grow_ref, m_out_ref = refs
    n = gates_ref.shape[2]
    col_blocks = []
    for d in range(2):
        rows8 = slice(8 * d, 8 * d + 8)
        m_init = m_init_ref[0, rows8, 0:1] if has_init else jnp.zeros((8, 1), F32)
        rows, g2, m_fin = _gate_direction(gates_ref[0, rows8, :], m_init, l, backward=d == 1)
        for k, v in enumerate(rows):
            for c in range(n // l):
                grow_ref[0, c, k * N_GATE_COLS + 8 * d:k * N_GATE_COLS + 8 * d + 8, :] = v[:, c * l:(c + 1) * l]
        col_blocks.append(g2)
        m_out_ref[0, rows8, :] = jnp.broadcast_to(m_fin, (8, GATE_LANES))
    col_src = jnp.concatenate(col_blocks + [jnp.zeros((GATE_LANES - N_GATE_COLS, n), F32)], axis=0)
    gcol_ref[0] = col_src.T


def _gateprep(gates, l, m_init):
    b, _, n = gates.shape
    assert l & (l - 1) == 0 and n % l == 0
    in_specs = [pl.BlockSpec((1, N_GATE_COLS, n), lambda bi: (bi, 0, 0))]
    args = [gates]
    if m_init is not None:
        in_specs.append(pl.BlockSpec((1, N_GATE_COLS, GATE_LANES), lambda bi: (bi, 0, 0)))
        args.append(m_init)
    return pl.pallas_call(
        functools.partial(_gateprep_kernel, l, m_init is not None),
        grid=(b,),
        in_specs=in_specs,
        out_specs=[pl.BlockSpec((1, n, GATE_LANES), lambda bi: (bi, 0, 0)),
                   pl.BlockSpec((1, n // l, GROW_ROWS, l), lambda bi: (bi, 0, 0, 0)),
                   pl.BlockSpec((1, N_GATE_COLS, GATE_LANES), lambda bi: (bi, 0, 0))],
        out_shape=[jax.ShapeDtypeStruct((b, n, GATE_LANES), F32),
                   jax.ShapeDtypeStruct((b, n // l, GROW_ROWS, l), F32),
                   jax.ShapeDtypeStruct((b, N_GATE_COLS, GATE_LANES), F32)],
        compiler_params=_params(("arbitrary",)),
        name="gateprep",
    )(*args)


STATE_PAD = 16
CHAIN_GROUP = 2
SCAN_STEP_CHUNKS = 4


def _mlstm_kernel(has_init, emit_h, emit_state, *refs):
    refs = list(refs)
    data = [refs[0:5], refs[5:10]]
    pos = 10
    if has_init:
        s_init_ref = refs[pos]
        pos += 1
    if emit_h:
        h_refs = refs[pos:pos + 2]
        pos += 2
    if emit_state:
        s_out_ref = refs[pos]
        pos += 1
    st_s, stb_s = refs[pos:pos + 2]

    j = pl.program_id(1)
    nj = pl.num_programs(1)
    cps, l = data[0][2].shape[1], data[0][2].shape[3]
    dh = data[0][1].shape[2] // M_HEADS

    @pl.when(j == 0)
    def _init():
        if has_init:
            st_s[...] = s_init_ref[0]
            stb_s[...] = s_init_ref[0].astype(BF16)
        else:
            st_s[...] = jnp.zeros_like(st_s)
            stb_s[...] = jnp.zeros_like(stb_s)

    r = lax.broadcasted_iota(jnp.int32, (l, l), 0)
    c = lax.broadcasted_iota(jnp.int32, (l, l), 1)
    neg = jnp.float32(-jnp.inf)
    ones_pad = jnp.ones((STATE_PAD, l), BF16)

    chains = [(s if d == 0 else cps - 1 - s, d, h) for s in range(cps) for d in range(2) for h in range(M_HEADS)]
    visible = [r <= c, r >= c]

    def dot_nt(a, b_mat):
        return lax.dot_general(a, b_mat, (((1,), (1,)), ((), ())), preferred_element_type=F32)

    def q_of(c, d, h):
        return data[d][0][0, c * l:(c + 1) * l, h * dh:(h + 1) * dh]

    def k_of(c, d, h):
        return data[d][1][0, c * l:(c + 1) * l, h * dh:(h + 1) * dh]

    def vt_of(c, d, h):
        return data[d][2][0, c, h * dh:(h + 1) * dh, :]

    def g_col(c, d, h):
        row = 8 * d + 4 + h
        return data[d][3][0, c * l:(c + 1) * l, row:row + 1]

    def grow(block, c, d, h):
        row = block * N_GATE_COLS + 8 * d + 4 + h
        return data[d][4][0, c, row:row + 1, :]

    for g0 in range(0, len(chains), CHAIN_GROUP):
        group = chains[g0:g0 + CHAIN_GROUP]
        if emit_h:
            qk_t = [dot_nt(k_of(c, d, h), q_of(c, d, h)) for c, d, h in group]
            inter = [dot_nt(stb_s[d * M_HEADS + h], q_of(c, d, h)) for c, d, h in group]
            s_t = [(qk_t[i] * jnp.exp2(jnp.where(visible[d], g_col(c, d, h) - grow(0, c, d, h), neg))
                    ).astype(BF16) for i, (c, d, h) in enumerate(group)]
            intra = [jnp.dot(jnp.concatenate([vt_of(c, d, h), ones_pad], axis=0), s_t[i],
                             preferred_element_type=F32) for i, (c, d, h) in enumerate(group)]
        upd = []
        for c, d, h in group:
            ws_b = grow(2, c, d, h).astype(BF16)
            lhs = jnp.concatenate([vt_of(c, d, h) * ws_b, jnp.broadcast_to(ws_b, (STATE_PAD, l))], axis=0)
            upd.append(jnp.dot(lhs, k_of(c, d, h), preferred_element_type=F32))
        for i, (c, d, h) in enumerate(group):
            if emit_h:
                m0 = grow(3, c, d, h)[:, 0:1]
                sa_row = jnp.exp2(m0 - grow(0, c, d, h))
                both = intra[i] + sa_row * inter[i]
                den = both[dh:dh + 1, :]
                h_t = both[0:dh, :] * (1.0 / jnp.maximum(jnp.abs(den), grow(1, c, d, h)))
                h_refs[d][0, c * l:(c + 1) * l, h * dh:(h + 1) * dh] = h_t.astype(BF16).T
            s0 = grow(4, c, d, h)[:, 0:1]
            st_new = s0 * st_s[d * M_HEADS + h] + upd[i]
            st_s[d * M_HEADS + h] = st_new
            stb_s[d * M_HEADS + h] = st_new.astype(BF16)

    if emit_state:
        @pl.when(j == nj - 1)
        def _emit():
            s_out_ref[0] = st_s[...]


def _mlstm(proj, q_col, k_col, vt, gcol, grow, l, init, emit_h, emit_state):
    b, _, dm, _ = vt.shape
    n = proj.shape[1]
    nc = n // l
    cps = SCAN_STEP_CHUNKS if nc % SCAN_STEP_CHUNKS == 0 else 1
    nb = nc // cps
    dh = dm // M_HEADS

    def chunk_slab(rows, backward):
        return pl.BlockSpec((1, cps, rows, l), lambda bi, j: (bi, nb - 1 - j if backward else j, 0, 0))

    def token_major(width, col, backward):
        return pl.BlockSpec((1, cps * l, width), lambda bi, j: (bi, nb - 1 - j if backward else j, col))

    in_specs, args = [], []
    for backward in (False, True):
        in_specs += [token_major(dm, q_col, backward), token_major(dm, k_col, backward), chunk_slab(dm, backward),
                     token_major(GATE_LANES, 0, backward), chunk_slab(GROW_ROWS, backward)]
        args += [proj, proj, vt, gcol, grow]
    state_shape = (2 * M_HEADS, dh + STATE_PAD, dh)
    state_spec = pl.BlockSpec((1,) + state_shape, lambda bi, j: (bi, 0, 0, 0))
    if init is not None:
        in_specs.append(state_spec)
        args.append(init)
    out_specs, out_shape = [], []
    if emit_h:
        out_specs += [token_major(dm, 0, False), token_major(dm, 0, True)]
        out_shape += [jax.ShapeDtypeStruct((b, n, dm), BF16)] * 2
    if emit_state:
        out_specs.append(state_spec)
        out_shape.append(jax.ShapeDtypeStruct((b,) + state_shape, F32))
    return pl.pallas_call(
        functools.partial(_mlstm_kernel, init is not None, emit_h, emit_state),
        grid=(b, nb),
        in_specs=in_specs,
        out_specs=out_specs,
        out_shape=out_shape,
        scratch_shapes=[pltpu.VMEM(state_shape, F32), pltpu.VMEM(state_shape, BF16)],
        compiler_params=_params(("arbitrary", "arbitrary")),
        name="mlstm",
    )(*args)


BAND = 256
POOL_RING = 3


def _band_matrices():
    t = np.arange(BAND)
    same_row = (t[:, None] // GRID_W) == (t[None, :] // GRID_W)
    ct, cs = t[:, None] % GRID_W, t[None, :] % GRID_W
    mats = [same_row & (cs >= ct - w // 2) & (cs <= ct + w // 2 - 1) for w in POOL_WINDOWS]
    return jnp.asarray(np.stack(mats).astype(np.float32), dtype=BF16)


def _pool_tile(u_tiles, sg_ref, band_ref, wp_ref, ps_ref, y_ref, cs_ref, p, nt, prev_ok, next_ok):
    up_ref, uc_ref, un_ref = u_tiles
    t = uc_ref.shape[0]
    gdim = wp_ref.shape[1]
    rt = t // GRID_W
    shift = GRID_W.bit_length() - 1
    tok = lax.broadcasted_iota(jnp.int32, (t, 1), 0)
    row = p * rt + lax.shift_right_logical(tok, shift)
    col = tok & (GRID_W - 1)

    for g, w in enumerate(POOL_WINDOWS):
        hw = w // 2
        cols = slice(g * gdim, (g + 1) * gdim)
        band = band_ref[g]
        lo = (rt - hw) * GRID_W // BAND * BAND
        hi = -(-((2 * rt + hw - 1) * GRID_W) // BAND) * BAND
        for b0 in range(lo, hi, BAND):
            seg, off = divmod(b0, t)
            if seg == 0:
                blk = jnp.where(prev_ok, up_ref[off:off + BAND, cols], 0)
            elif seg == 1:
                blk = uc_ref[off:off + BAND, cols]
            else:
                blk = jnp.where(next_ok, un_ref[off:off + BAND, cols], 0)
            cs_ref[b0:b0 + BAND, :] = jnp.dot(band, blk, preferred_element_type=F32)
        acc = cs_ref[(rt - hw) * GRID_W:(rt - hw) * GRID_W + t, :]
        for jr in range(-hw + 1, hw):
            acc = acc + cs_ref[(rt + jr) * GRID_W:(rt + jr) * GRID_W + t, :]
        cnt_r = jnp.minimum(row + hw, nt * rt) - jnp.maximum(row - hw, 0)
        cnt_c = jnp.minimum(col + hw, GRID_W) - jnp.maximum(col - hw, 0)
        inv = 1.0 / (cnt_r * cnt_c).astype(F32)
        delta = acc * inv - uc_ref[:, cols].astype(F32)
        po = jnp.dot(delta.astype(BF16), wp_ref[g], preferred_element_type=F32)
        y_ref[0, :, cols] = (sg_ref[:, cols].astype(F32) * (po * ps_ref[:, cols])).astype(BF16)


FFN_CHUNK = 256


def _tail_kernel(x_ref, hf_ref, hb_ref, yp_ref, so_ref, sgm_ref, g1_ref, sh2_ref, sc2_ref, g2_ref,
                 mhg_ref, wo_ref, nf_ref, wi_ref, wo2_ref, nfin_ref, o_ref):
    d = x_ref.shape[2]
    dh = d // M_HEADS
    f = wo2_ref.shape[0]

    hs = hf_ref[0].astype(F32) + hb_ref[0].astype(F32)
    parts = []
    for h in range(M_HEADS):
        hh = hs[:, h * dh:(h + 1) * dh]
        parts.append(hh * _rms_scale(hh, dh))
    m_out = jnp.concatenate(parts, axis=-1) * mhg_ref[...] * so_ref[0].astype(F32)
    y = yp_ref[0].astype(F32) + sgm_ref[0].astype(F32) * m_out
    mix = jnp.dot(y.astype(BF16), wo_ref[...], preferred_element_type=F32)
    x1 = x_ref[0] + g1_ref[0] * mix

    a2 = ((x1 * _rms_scale(x1, d)) * nf_ref[...] * (1.0 + sc2_ref[0]) + sh2_ref[0]).astype(BF16)
    acc = jnp.zeros_like(x1)
    for c0 in range(0, f, FFN_CHUNK):
        c1 = min(c0 + FFN_CHUNK, f)
        gate = jnp.dot(a2, wi_ref[:, c0:c1], preferred_element_type=F32)
        up = jnp.dot(a2, wi_ref[:, f + c0:f + c1], preferred_element_type=F32)
        act = (_silu(gate) * up).astype(BF16)
        acc = acc + jnp.dot(act, wo2_ref[c0:c1, :], preferred_element_type=F32)
    x2 = x1 + g2_ref[0] * acc
    o_ref[0] = (x2 * _rms_scale(x2, d)) * nfin_ref[...]


def _tail(x, hf, hb, ypool, proj, so_col, sgm_col, mod, mh_gain, w_out, norm_ffn, w_ffn_in, w_ffn_out,
          norm_final):
    b, n, d = x.shape
    t = min(TOKEN_TILE, n)
    assert w_ffn_out.shape[0] % 256 == 0 and FFN_CHUNK % 256 == 0

    def tile(col=0):
        return pl.BlockSpec((1, t, d), lambda bi, i: (bi, i, col))

    def mod_spec(col):
        return pl.BlockSpec((1, 1, d), lambda bi, i: (bi, 0, col))

    return pl.pallas_call(
        _tail_kernel,
        grid=(b, n // t),
        in_specs=[tile(), tile(), tile(), tile(), tile(so_col), tile(sgm_col),
                  mod_spec(2), mod_spec(3), mod_spec(4), mod_spec(5),
                  _resident(mh_gain.shape), _resident(w_out.shape), _resident(norm_ffn.shape),
                  _resident(w_ffn_in.shape), _resident(w_ffn_out.shape), _resident(norm_final.shape)],
        out_specs=tile(),
        out_shape=jax.ShapeDtypeStruct((b, n, d), F32),
        compiler_params=_params(("arbitrary", "arbitrary")),
        name="tail",
    )(x, hf, hb, ypool, proj, proj, mod, mod, mod, mod, mh_gain, w_out, norm_ffn, w_ffn_in, w_ffn_out,
      norm_final)


def kernel(x, c, ctx, c_ctx, norm_mix, norm_ffn, norm_final, w_ada, b_ada, w_in, b_gates, conv_w, conv_b,
           w_pool, pool_scale, mh_gain, w_out, w_ffn_in, w_ffn_out):
    assert w_ada.shape[0] == 1, "single-layer stack only"
    b, n, d = x.shape
    assert d % (128 * M_HEADS) == 0 and w_in.shape[2] == 7 * d + N_GATE_COLS
    dh = d // M_HEADS

    pad_rows = -(b + 1) % 8
    c_rows = jnp.concatenate([c, c_ctx[None], jnp.zeros((pad_rows, d), F32)], axis=0)
    mod_all = _ada(c_rows, w_ada[0], b_ada)
    mod = mod_all[:b, None, :]
    mod_ctx = mod_all[b:b + 1, None, :]

    w_main = _cast_bf16_from_transposed(jnp.swapaxes(w_in[0], 0, 1), 7 * d)
    w_gate = jnp.pad(w_in[0, :, 7 * d:], ((0, 0), (0, GATE_LANES - N_GATE_COLS))).astype(BF16)
    b_gate = jnp.pad(b_gates, ((0, 0), (0, GATE_LANES - N_GATE_COLS)))
    q_scale = dh ** -0.5

    groups, gdim = w_pool.shape[1], w_pool.shape[2]
    w_pool_b = _cast_bf16(w_pool[0].reshape(groups * gdim, gdim)).reshape(groups, gdim, gdim)

    lat_kinds = (("plain", "pool_u"), ("q", "proj"), ("k", "proj"), ("plain", "tr"),
                 ("sig", "proj"), ("sig", "pool_gate"), ("sig", "proj"))
    ctx_kinds = (("skip", ""), ("q", "proj"), ("k", "proj"), ("plain", "tr"))
    l_lat = min(SCAN_CHUNK, n)
    l_ctx = min(SCAN_CHUNK, ctx.shape[1])
    proj, vt, gates, ypool = _in_proj(x, mod, (0, 1), norm_mix, w_main, w_gate, conv_w[0], conv_b, b_gate,
                                      lat_kinds, q_scale, l_lat, pool=(_band_matrices(), w_pool_b, pool_scale))
    proj_c, vt_c, gates_c = _in_proj(ctx, mod_ctx, (0, 1), norm_mix, w_main, w_gate, conv_w[0],
                                     conv_b, b_gate, ctx_kinds, q_scale, l_ctx)

    gcol_c, grow_c, m_c = _gateprep(gates_c, l_ctx, None)
    gcol, grow, _ = _gateprep(gates, l_lat, m_c)
    (state,) = _mlstm(proj_c, 0, 1, vt_c, gcol_c, grow_c, l_ctx, None, False, True)
    h_dirs = _mlstm(proj, 0, 1, vt, gcol, grow, l_lat, state, True, False)

    return _tail(x, h_dirs[0], h_dirs[1], ypool, proj, 2, 3, mod, mh_gain, _cast_bf16(w_out[0]), norm_ffn,
                 _cast_bf16(w_ffn_in[0]), _cast_bf16(w_ffn_out[0]), norm_final[None])
```

```python
import functools
import math

import numpy as np
import jax
import jax.numpy as jnp
from jax import lax
from jax.experimental import pallas as pl
from jax.experimental.pallas import tpu as pltpu

F32 = jnp.float32
BF16 = jnp.bfloat16

EPS = 1e-6
LOG2E = math.log2(math.e)
GRID_W = 64
POOL_WINDOWS = (2, 4, 8, 16)
M_HEADS = 4
CONV_W = 3
N_GATE_COLS = 4 * M_HEADS
GATE_LANES = 128
SCAN_CHUNK = 256
TOKEN_TILE = 512
HALO_ROWS = 16
VMEM_LIMIT = 60 * 1024 * 1024


def _resident(shape):
    nd = len(shape)
    return pl.BlockSpec(shape, lambda *_: (0,) * nd, pipeline_mode=pl.Buffered(1))


def _params(sem):
    return pltpu.CompilerParams(dimension_semantics=sem, vmem_limit_bytes=VMEM_LIMIT)


def _sigmoid(x):
    return 0.5 + 0.5 * jnp.tanh(0.5 * x)


def _silu(x):
    half = 0.5 * x
    return half + half * jnp.tanh(half)


def _rms_scale(xf, d):
    return lax.rsqrt(jnp.sum(xf * xf, axis=-1, keepdims=True) * (1.0 / d) + EPS)


def _ada_kernel(c_ref, w_ref, b_ref, o_ref):
    s = _silu(c_ref[...])
    o_ref[...] = jnp.dot(s.astype(BF16), w_ref[...].astype(BF16), preferred_element_type=F32) + b_ref[...]


def _ada(c_rows, w_ada, b_ada):
    r, d = c_rows.shape
    n = w_ada.shape[1]
    return pl.pallas_call(
        _ada_kernel,
        grid=(n // d,),
        in_specs=[pl.BlockSpec((r, d), lambda j: (0, 0)),
                  pl.BlockSpec((d, d), lambda j: (0, j)),
                  pl.BlockSpec((1, d), lambda j: (0, j))],
        out_specs=pl.BlockSpec((r, d), lambda j: (0, j)),
        out_shape=jax.ShapeDtypeStruct((r, n), F32),
        compiler_params=_params(("arbitrary",)),
        name="ada",
    )(c_rows, w_ada, b_ada)


def _cast_kernel(w_ref, o_ref):
    o_ref[...] = w_ref[...].astype(BF16)


def _cast_bf16(w, ncols=None):
    k, n = w.shape
    ncols = n if ncols is None else ncols
    block = max(bw for bw in range(128, min(ncols, 1024) + 1, 128) if ncols % bw == 0)
    return pl.pallas_call(
        _cast_kernel,
        grid=(ncols // block,),
        in_specs=[pl.BlockSpec((k, block), lambda j: (0, j))],
        out_specs=pl.BlockSpec((k, block), lambda j: (0, j)),
        out_shape=jax.ShapeDtypeStruct((k, ncols), BF16),
        compiler_params=_params(("arbitrary",)),
        name="wcast",
    )(w)


def _cast_t_kernel(w_ref, o_ref):
    o_ref[...] = w_ref[...].T.astype(BF16)


def _cast_bf16_from_transposed(wt, ncols, block=1024):
    _, k = wt.shape
    assert ncols % block == 0
    return pl.pallas_call(
        _cast_t_kernel,
        grid=(ncols // block,),
        in_specs=[pl.BlockSpec((block, k), lambda j: (j, 0))],
        out_specs=pl.BlockSpec((k, block), lambda j: (0, j)),
        out_shape=jax.ShapeDtypeStruct((k, ncols), BF16),
        compiler_params=_params(("arbitrary",)),
        name="wcast_t",
    )(wt)


def _in_proj_kernel(kinds, q_scale, with_pool, xp_ref, xc_ref, xn_ref, sh_ref, sc_ref, gain_ref, w_ref, wg_ref,
                    cw_ref, cb_ref, bg_ref, *rest):
    rest = list(rest)
    if with_pool:
        band_ref, wp_ref, ps_ref = rest[:3]
        rest = rest[3:]
    n_tr = sum(dest == "tr" for _, dest in kinds)
    proj_ref = rest[0]
    tr_refs = rest[1:1 + n_tr]
    gates_ref = rest[1 + n_tr]
    rest = rest[2 + n_tr:]
    if with_pool:
        ypool_ref, qk_ref, u_ring, sg_ring, cs_ref = rest
    else:
        (qk_ref,) = rest
    i = pl.program_id(1)
    nt = pl.num_programs(1) - (1 if with_pool else 0)
    t = xc_ref.shape[1]
    d = xc_ref.shape[2]

    def project():
        gain = gain_ref[...]
        mult = 1.0 + sc_ref[0]
        shift = sh_ref[0]

        def norm_mod(xf):
            return (xf * _rms_scale(xf, d)) * gain * mult + shift

        prev = jnp.where(i > 0, norm_mod(xp_ref[0]), 0.0)
        nxt = jnp.where(i < nt - 1, norm_mod(xn_ref[0]), 0.0)
        zeros8 = jnp.zeros((HALO_ROWS - 8, d), F32)
        a_main = norm_mod(xc_ref[0]).astype(BF16)
        a_ext = jnp.concatenate([jnp.concatenate([zeros8, prev], axis=0).astype(BF16), a_main,
                                 jnp.concatenate([nxt, zeros8], axis=0).astype(BF16)], axis=0)

        out_col = 0
        out_tr = 0
        for g, (kind, dest) in enumerate(kinds):
            cols = slice(g * d, (g + 1) * d)
            if kind == "skip":
                continue
            if kind in ("q", "k"):
                qk_ref[...] = jnp.dot(a_ext, w_ref[:, cols], preferred_element_type=F32)
                c0 = 0 if kind == "q" else d
                cw = cw_ref[:, c0:c0 + d]
                y = (cb_ref[:, c0:c0 + d]
                     + qk_ref[HALO_ROWS - 1:HALO_ROWS - 1 + t, :] * cw[0:1]
                     + qk_ref[HALO_ROWS:HALO_ROWS + t, :] * cw[1:2]
                     + qk_ref[HALO_ROWS + 1:HALO_ROWS + 1 + t, :] * cw[2:3])
                y = _silu(y)
                if kind == "q":
                    y = y * q_scale
            else:
                y = jnp.dot(a_main, w_ref[:, cols], preferred_element_type=F32)
                if kind == "sig":
                    y = _sigmoid(y)
            if dest == "tr":
                y_t = y.T.astype(BF16)
                l = tr_refs[out_tr].shape[3]
                for c in range(t // l):
                    tr_refs[out_tr][0, c] = y_t[:, c * l:(c + 1) * l]
                out_tr += 1
            elif dest == "pool_u":
                u_ring[lax.rem(i, POOL_RING)] = y.astype(BF16)
            elif dest == "pool_gate":
                sg_ring[lax.rem(i, 2)] = y.astype(BF16)
            else:
                proj_ref[0, :, out_col * d:(out_col + 1) * d] = y.astype(BF16)
                out_col += 1

        gates = jnp.dot(a_main, wg_ref[...], preferred_element_type=F32) + bg_ref[...]
        gates_ref[0] = gates.T[0:N_GATE_COLS, :]

    if not with_pool:
        project()
        return

    def pool():
        p = i - 1
        u_tiles = [u_ring.at[lax.rem(p + POOL_RING - 1, POOL_RING)], u_ring.at[lax.rem(p, POOL_RING)],
                   u_ring.at[lax.rem(p + 1, POOL_RING)]]
        _pool_tile(u_tiles, sg_ring.at[lax.rem(p, 2)], band_ref, wp_ref, ps_ref, ypool_ref, cs_ref,
                   p, nt, p >= 1, p < nt - 1)

    pl.when(i < nt)(project)
    pl.when(i >= 1)(pool)


def _in_proj(x, mod, mod_cols, gain, w, wg, conv_w, conv_b, bg, kinds, q_scale, l, pool=None):
    b, n, d = x.shape
    t = min(TOKEN_TILE, n)
    nt = n // t
    per_batch = mod.shape[0] > 1
    sh_col, sc_col = mod_cols
    with_pool = pool is not None

    def tile_of(i):
        return jnp.minimum(i, nt - 1)

    def mod_map(col):
        return lambda bi, i: (bi if per_batch else 0, 0, col)

    t8 = t // 8
    last8 = n // 8 - 1
    in_specs = [
        pl.BlockSpec((1, 8, d), lambda bi, i: (bi, jnp.maximum(tile_of(i) * t8 - 1, 0), 0)),
        pl.BlockSpec((1, t, d), lambda bi, i: (bi, tile_of(i), 0)),
        pl.BlockSpec((1, 8, d), lambda bi, i: (bi, jnp.minimum((tile_of(i) + 1) * t8, last8), 0)),
        pl.BlockSpec((1, 1, d), mod_map(sh_col)),
        pl.BlockSpec((1, 1, d), mod_map(sc_col)),
        _resident(gain.shape), _resident((d, len(kinds) * d)), _resident(wg.shape),
        _resident(conv_w.shape), _resident(conv_b.shape), _resident(bg.shape),
    ]
    args = [x, x, x, mod, mod, gain, w, wg, conv_w, conv_b, bg]
    n_tr = sum(dest == "tr" for _, dest in kinds)
    ncol = sum(kind != "skip" and dest == "proj" for kind, dest in kinds) * d
    out_specs = ([pl.BlockSpec((1, t, ncol), lambda bi, i: (bi, tile_of(i), 0))]
                 + [pl.BlockSpec((1, t // l, d, l), lambda bi, i: (bi, tile_of(i), 0, 0))] * n_tr
                 + [pl.BlockSpec((1, N_GATE_COLS, t), lambda bi, i: (bi, 0, tile_of(i)))])
    out_shape = ([jax.ShapeDtypeStruct((b, n, ncol), BF16)]
                 + [jax.ShapeDtypeStruct((b, n // l, d, l), BF16)] * n_tr
                 + [jax.ShapeDtypeStruct((b, N_GATE_COLS, n), F32)])
    scratch = [pltpu.VMEM((t + 2 * HALO_ROWS, d), F32)]
    if with_pool:
        assert t == TOKEN_TILE and n % t == 0 and t % BAND == 0 and BAND % GRID_W == 0
        assert GRID_W & (GRID_W - 1) == 0 and t // GRID_W >= max(POOL_WINDOWS) // 2
        in_specs += [_resident(a.shape) for a in pool]
        args += list(pool)
        out_specs.append(pl.BlockSpec((1, t, d), lambda bi, i: (bi, jnp.maximum(i - 1, 0), 0)))
        out_shape.append(jax.ShapeDtypeStruct((b, n, d), BF16))
        scratch += [pltpu.VMEM((POOL_RING, t, d), BF16), pltpu.VMEM((2, t, d), BF16),
                    pltpu.VMEM((3 * t, d // len(POOL_WINDOWS)), F32)]
    return pl.pallas_call(
        functools.partial(_in_proj_kernel, kinds, q_scale, with_pool),
        grid=(b, nt + 1 if with_pool else nt),
        in_specs=in_specs,
        out_specs=out_specs,
        out_shape=out_shape,
        scratch_shapes=scratch,
        compiler_params=_params(("arbitrary", "arbitrary")),
        name="in_proj",
    )(*args)


GROW_BLOCKS = 5
GROW_ROWS = GROW_BLOCKS * N_GATE_COLS


def _gate_direction(x, m_init, l, backward):
    n = x.shape[1]
    nc = n // l
    pos = lax.broadcasted_iota(jnp.int32, x.shape, 1) & (l - 1)
    neg = jnp.float32(-jnp.inf)

    def shifted(v, s, fill):
        if backward:
            return jnp.where(pos < l - s, pltpu.roll(v, n - s, axis=1), fill)
        return jnp.where(pos >= s, pltpu.roll(v, s, axis=1), fill)

    csum = jnp.minimum(x, 0.0) - jnp.log1p(jnp.exp(-jnp.abs(x)))
    s = 1
    while s < l:
        csum = csum + shifted(csum, s, 0.0)
        s *= 2
    g = pltpu.roll(x, 4, axis=0) - csum
    gmax = g
    s = 1
    while s < l:
        gmax = jnp.maximum(gmax, shifted(gmax, s, neg))
        s *= 2

    m0_parts, ml_parts = [None] * nc, [None] * nc
    m0 = m_init
    for step in range(nc):
        ci = nc - 1 - step if backward else step
        last = ci * l if backward else ci * l + l - 1
        m_last = jnp.maximum(m0, gmax[:, last:last + 1])
        m0_parts[ci] = jnp.broadcast_to(m0, (8, l))
        ml_parts[ci] = jnp.broadcast_to(m_last, (8, l))
        m0 = csum[:, last:last + 1] + m_last
    m0_arr = jnp.concatenate(m0_parts, axis=1)
    ml_arr = jnp.concatenate(ml_parts, axis=1)
    big_m = jnp.maximum(m0_arr, gmax)
    f_rows = lax.broadcasted_iota(jnp.int32, x.shape, 0) >= 4
    rows = (big_m * LOG2E, jnp.exp(-(csum + big_m)), jnp.exp(g - ml_arr), m0_arr * LOG2E,
            jnp.exp(m0_arr - ml_arr))
    rows = tuple(jnp.where(f_rows, v, 0.0) for v in rows)
    return rows, jnp.where(f_rows, g * LOG2E, 0.0), m0


def _gateprep_kernel(l, has_init, *refs):
    refs = list(refs)
    gates_ref = refs.pop(0)
    m_init_ref = refs.pop(0) if has_init else None
    gcol_ref, grow_ref, m_out_ref = refs
    n = gates_ref.shape[2]
    col_blocks = []
    for d in range(2):
        rows8 = slice(8 * d, 8 * d + 8)
        m_init = m_init_ref[0, rows8, 0:1] if has_init else jnp.zeros((8, 1), F32)
        rows, g2, m_fin = _gate_direction(gates_ref[0, rows8, :], m_init, l, backward=d == 1)
        for k, v in enumerate(rows):
            for c in range(n // l):
                grow_ref[0, c, k * N_GATE_COLS + 8 * d:k * N_GATE_COLS + 8 * d + 8, :] = v[:, c * l:(c + 1) * l]
        col_blocks.append(g2)
        m_out_ref[0, rows8, :] = jnp.broadcast_to(m_fin, (8, GATE_LANES))
    col_src = jnp.concatenate(col_blocks + [jnp.zeros((GATE_LANES - N_GATE_COLS, n), F32)], axis=0)
    gcol_ref[0] = col_src.T


def _gateprep(gates, l, m_init):
    b, _, n = gates.shape
    assert l & (l - 1) == 0 and n % l == 0
    in_specs = [pl.BlockSpec((1, N_GATE_COLS, n), lambda bi: (bi, 0, 0))]
    args = [gates]
    if m_init is not None:
        in_specs.append(pl.BlockSpec((1, N_GATE_COLS, GATE_LANES), lambda bi: (bi, 0, 0)))
        args.append(m_init)
    return pl.pallas_call(
        functools.partial(_gateprep_kernel, l, m_init is not None),
        grid=(b,),
        in_specs=in_specs,
        out_specs=[pl.BlockSpec((1, n, GATE_LANES), lambda bi: (bi, 0, 0)),
                   pl.BlockSpec((1, n // l, GROW_ROWS, l), lambda bi: (bi, 0, 0, 0)),
                   pl.BlockSpec((1, N_GATE_COLS, GATE_LANES), lambda bi: (bi, 0, 0))],
        out_shape=[jax.ShapeDtypeStruct((b, n, GATE_LANES), F32),
                   jax.ShapeDtypeStruct((b, n // l, GROW_ROWS, l), F32),
                   jax.ShapeDtypeStruct((b, N_GATE_COLS, GATE_LANES), F32)],
        compiler_params=_params(("arbitrary",)),
        name="gateprep",
    )(*args)


STATE_PAD = 16
CHAIN_GROUP = 2
SCAN_STEP_CHUNKS = 4


def _mlstm_kernel(has_init, emit_h, emit_state, *refs):
    refs = list(refs)
    data = [refs[0:5], refs[5:10]]
    pos = 10
    if has_init:
        s_init_ref = refs[pos]
        pos += 1
    if emit_h:
        h_refs = refs[pos:pos + 2]
        pos += 2
    if emit_state:
        s_out_ref = refs[pos]
        pos += 1
    st_s, stb_s = refs[pos:pos + 2]

    j = pl.program_id(1)
    nj = pl.num_programs(1)
    cps, l = data[0][2].shape[1], data[0][2].shape[3]
    dh = data[0][1].shape[2] // M_HEADS

    @pl.when(j == 0)
    def _init():
        if has_init:
            st_s[...] = s_init_ref[0]
            stb_s[...] = s_init_ref[0].astype(BF16)
        else:
            st_s[...] = jnp.zeros_like(st_s)
            stb_s[...] = jnp.zeros_like(stb_s)

    r = lax.broadcasted_iota(jnp.int32, (l, l), 0)
    c = lax.broadcasted_iota(jnp.int32, (l, l), 1)
    neg = jnp.float32(-jnp.inf)
    ones_pad = jnp.ones((STATE_PAD, l), BF16)

    chains = [(s if d == 0 else cps - 1 - s, d, h) for s in range(cps) for d in range(2) for h in range(M_HEADS)]
    visible = [r <= c, r >= c]

    def dot_nt(a, b_mat):
        return lax.dot_general(a, b_mat, (((1,), (1,)), ((), ())), preferred_element_type=F32)

    def q_of(c, d, h):
        return data[d][0][0, c * l:(c + 1) * l, h * dh:(h + 1) * dh]

    def k_of(c, d, h):
        return data[d][1][0, c * l:(c + 1) * l, h * dh:(h + 1) * dh]

    def vt_of(c, d, h):
        return data[d][2][0, c, h * dh:(h + 1) * dh, :]

    def g_col(c, d, h):
        row = 8 * d + 4 + h
        return data[d][3][0, c * l:(c + 1) * l, row:row + 1]

    def grow(block, c, d, h):
        row = block * N_GATE_COLS + 8 * d + 4 + h
        return data[d][4][0, c, row:row + 1, :]

    for g0 in range(0, len(chains), CHAIN_GROUP):
        group = chains[g0:g0 + CHAIN_GROUP]
        if emit_h:
            qk_t = [dot_nt(k_of(c, d, h), q_of(c, d, h)) for c, d, h in group]
            inter = [dot_nt(stb_s[d * M_HEADS + h], q_of(c, d, h)) for c, d, h in group]
            s_t = [(qk_t[i] * jnp.exp2(jnp.where(visible[d], g_col(c, d, h) - grow(0, c, d, h), neg))
                    ).astype(BF16) for i, (c, d, h) in enumerate(group)]
            intra = [jnp.dot(jnp.concatenate([vt_of(c, d, h), ones_pad], axis=0), s_t[i],
                             preferred_element_type=F32) for i, (c, d, h) in enumerate(group)]
        upd = []
        for c, d, h in group:
            ws_b = grow(2, c, d, h).astype(BF16)
            lhs = jnp.concatenate([vt_of(c, d, h) * ws_b, jnp.broadcast_to(ws_b, (STATE_PAD, l))], axis=0)
            upd.append(jnp.dot(lhs, k_of(c, d, h), preferred_element_type=F32))
        for i, (c, d, h) in enumerate(group):
            if emit_h:
                m0 = grow(3, c, d, h)[:, 0:1]
                sa_row = jnp.exp2(m0 - grow(0, c, d, h))
                both = intra[i] + sa_row * inter[i]
                den = both[dh:dh + 1, :]
                h_t = both[0:dh, :] * (1.0 / jnp.maximum(jnp.abs(den), grow(1, c, d, h)))
                h_refs[d][0, c * l:(c + 1) * l, h * dh:(h + 1) * dh] = h_t.astype(BF16).T
            s0 = grow(4, c, d, h)[:, 0:1]
            st_new = s0 * st_s[d * M_HEADS + h] + upd[i]
            st_s[d * M_HEADS + h] = st_new
            stb_s[d * M_HEADS + h] = st_new.astype(BF16)

    if emit_state:
        @pl.when(j == nj - 1)
        def _emit():
            s_out_ref[0] = st_s[...]


def _mlstm(proj, q_col, k_col, vt, gcol, grow, l, init, emit_h, emit_state):
    b, _, dm, _ = vt.shape
    n = proj.shape[1]
    nc = n // l
    cps = SCAN_STEP_CHUNKS if nc % SCAN_STEP_CHUNKS == 0 else 1
    nb = nc // cps
    dh = dm // M_HEADS

    def chunk_slab(rows, backward):
        return pl.BlockSpec((1, cps, rows, l), lambda bi, j: (bi, nb - 1 - j if backward else j, 0, 0))

    def token_major(width, col, backward):
        return pl.BlockSpec((1, cps * l, width), lambda bi, j: (bi, nb - 1 - j if backward else j, col))

    in_specs, args = [], []
    for backward in (False, True):
        in_specs += [token_major(dm, q_col, backward), token_major(dm, k_col, backward), chunk_slab(dm, backward),
                     token_major(GATE_LANES, 0, backward), chunk_slab(GROW_ROWS, backward)]
        args += [proj, proj, vt, gcol, grow]
    state_shape = (2 * M_HEADS, dh + STATE_PAD, dh)
    state_spec = pl.BlockSpec((1,) + state_shape, lambda bi, j: (bi, 0, 0, 0))
    if init is not None:
        in_specs.append(state_spec)
        args.append(init)
    out_specs, out_shape = [], []
    if emit_h:
        out_specs += [token_major(dm, 0, False), token_major(dm, 0, True)]
        out_shape += [jax.ShapeDtypeStruct((b, n, dm), BF16)] * 2
    if emit_state:
        out_specs.append(state_spec)
        out_shape.append(jax.ShapeDtypeStruct((b,) + state_shape, F32))
    return pl.pallas_call(
        functools.partial(_mlstm_kernel, init is not None, emit_h, emit_state),
        grid=(b, nb),
        in_specs=in_specs,
        out_specs=out_specs,
        out_shape=out_shape,
        scratch_shapes=[pltpu.VMEM(state_shape, F32), pltpu.VMEM(state_shape, BF16)],
        compiler_params=_params(("arbitrary", "arbitrary")),
        name="mlstm",
    )(*args)


BAND = 256
POOL_RING = 3


def _band_matrices():
    t = np.arange(BAND)
    same_row = (t[:, None] // GRID_W) == (t[None, :] // GRID_W)
    ct, cs = t[:, None] % GRID_W, t[None, :] % GRID_W
    mats = [same_row & (cs >= ct - w // 2) & (cs <= ct + w // 2 - 1) for w in POOL_WINDOWS]
    return jnp.asarray(np.stack(mats).astype(np.float32), dtype=BF16)


def _pool_tile(u_tiles, sg_ref, band_ref, wp_ref, ps_ref, y_ref, cs_ref, p, nt, prev_ok, next_ok):
    up_ref, uc_ref, un_ref = u_tiles
    t = uc_ref.shape[0]
    gdim = wp_ref.shape[1]
    rt = t // GRID_W
    shift = GRID_W.bit_length() - 1
    tok = lax.broadcasted_iota(jnp.int32, (t, 1), 0)
    row = p * rt + lax.shift_right_logical(tok, shift)
    col = tok & (GRID_W - 1)

    for g, w in enumerate(POOL_WINDOWS):
        hw = w // 2
        cols = slice(g * gdim, (g + 1) * gdim)
        band = band_ref[g]
        lo = (rt - hw) * GRID_W // BAND * BAND
        hi = -(-((2 * rt + hw - 1) * GRID_W) // BAND) * BAND
        for b0 in range(lo, hi, BAND):
            seg, off = divmod(b0, t)
            if seg == 0:
                blk = jnp.where(prev_ok, up_ref[off:off + BAND, cols], 0)
            elif seg == 1:
                blk = uc_ref[off:off + BAND, cols]
            else:
                blk = jnp.where(next_ok, un_ref[off:off + BAND, cols], 0)
            cs_ref[b0:b0 + BAND, :] = jnp.dot(band, blk, preferred_element_type=F32)
        acc = cs_ref[(rt - hw) * GRID_W:(rt - hw) * GRID_W + t, :]
        for jr in range(-hw + 1, hw):
            acc = acc + cs_ref[(rt + jr) * GRID_W:(rt + jr) * GRID_W + t, :]
        cnt_r = jnp.minimum(row + hw, nt * rt) - jnp.maximum(row - hw, 0)
        cnt_c = jnp.minimum(col + hw, GRID_W) - jnp.maximum(col - hw, 0)
        inv = 1.0 / (cnt_r * cnt_c).astype(F32)
        delta = acc * inv - uc_ref[:, cols].astype(F32)
        po = jnp.dot(delta.astype(BF16), wp_ref[g], preferred_element_type=F32)
        y_ref[0, :, cols] = (sg_ref[:, cols].astype(F32) * (po * ps_ref[:, cols])).astype(BF16)


FFN_CHUNK = 256


def _tail_kernel(x_ref, hf_ref, hb_ref, yp_ref, so_ref, sgm_ref, g1_ref, sh2_ref, sc2_ref, g2_ref,
                 mhg_ref, wo_ref, nf_ref, wi_ref, wo2_ref, nfin_ref, o_ref):
    d = x_ref.shape[2]
    dh = d // M_HEADS
    f = wo2_ref.shape[0]

    hs = hf_ref[0].astype(F32) + hb_ref[0].astype(F32)
    parts = []
    for h in range(M_HEADS):
        hh = hs[:, h * dh:(h + 1) * dh]
        parts.append(hh * _rms_scale(hh, dh))
    m_out = jnp.concatenate(parts, axis=-1) * mhg_ref[...] * so_ref[0].astype(F32)
    y = yp_ref[0].astype(F32) + sgm_ref[0].astype(F32) * m_out
    mix = jnp.dot(y.astype(BF16), wo_ref[...], preferred_element_type=F32)
    x1 = x_ref[0] + g1_ref[0] * mix

    a2 = ((x1 * _rms_scale(x1, d)) * nf_ref[...] * (1.0 + sc2_ref[0]) + sh2_ref[0]).astype(BF16)
    acc = jnp.zeros_like(x1)
    for c0 in range(0, f, FFN_CHUNK):
        c1 = min(c0 + FFN_CHUNK, f)
        gate = jnp.dot(a2, wi_ref[:, c0:c1], preferred_element_type=F32)
        up = jnp.dot(a2, wi_ref[:, f + c0:f + c1], preferred_element_type=F32)
        act = (_silu(gate) * up).astype(BF16)
        acc = acc + jnp.dot(act, wo2_ref[c0:c1, :], preferred_element_type=F32)
    x2 = x1 + g2_ref[0] * acc
    o_ref[0] = (x2 * _rms_scale(x2, d)) * nfin_ref[...]


def _tail(x, hf, hb, ypool, proj, so_col, sgm_col, mod, mh_gain, w_out, norm_ffn, w_ffn_in, w_ffn_out,
          norm_final):
    b, n, d = x.shape
    t = min(TOKEN_TILE, n)
    assert w_ffn_out.shape[0] % 256 == 0 and FFN_CHUNK % 256 == 0

    def tile(col=0):
        return pl.BlockSpec((1, t, d), lambda bi, i: (bi, i, col))

    def mod_spec(col):
        return pl.BlockSpec((1, 1, d), lambda bi, i: (bi, 0, col))

    return pl.pallas_call(
        _tail_kernel,
        grid=(b, n // t),
        in_specs=[tile(), tile(), tile(), tile(), tile(so_col), tile(sgm_col),
                  mod_spec(2), mod_spec(3), mod_spec(4), mod_spec(5),
                  _resident(mh_gain.shape), _resident(w_out.shape), _resident(norm_ffn.shape),
                  _resident(w_ffn_in.shape), _resident(w_ffn_out.shape), _resident(norm_final.shape)],
        out_specs=tile(),
        out_shape=jax.ShapeDtypeStruct((b, n, d), F32),
        compiler_params=_params(("arbitrary", "arbitrary")),
        name="tail",
    )(x, hf, hb, ypool, proj, proj, mod, mod, mod, mod, mh_gain, w_out, norm_ffn, w_ffn_in, w_ffn_out,
      norm_final)


def kernel(x, c, ctx, c_ctx, norm_mix, norm_ffn, norm_final, w_ada, b_ada, w_in, b_gates, conv_w, conv_b,
           w_pool, pool_scale, mh_gain, w_out, w_ffn_in, w_ffn_out):
    assert w_ada.shape[0] == 1, "single-layer stack only"
    b, n, d = x.shape
    assert d % (128 * M_HEADS) == 0 and w_in.shape[2] == 7 * d + N_GATE_COLS
    dh = d // M_HEADS

    pad_rows = -(b + 1) % 8
    c_rows = jnp.concatenate([c, c_ctx[None], jnp.zeros((pad_rows, d), F32)], axis=0)
    mod_all = _ada(c_rows, w_ada[0], b_ada)
    mod = mod_all[:b, None, :]
    mod_ctx = mod_all[b:b + 1, None, :]

    w_main = _cast_bf16_from_transposed(jnp.swapaxes(w_in[0], 0, 1), 7 * d)
    w_gate = jnp.pad(w_in[0, :, 7 * d:], ((0, 0), (0, GATE_LANES - N_GATE_COLS))).astype(BF16)
    b_gate = jnp.pad(b_gates, ((0, 0), (0, GATE_LANES - N_GATE_COLS)))
    q_scale = dh ** -0.5

    groups, gdim = w_pool.shape[1], w_pool.shape[2]
    w_pool_b = _cast_bf16(w_pool[0].reshape(groups * gdim, gdim)).reshape(groups, gdim, gdim)

    lat_kinds = (("plain", "pool_u"), ("q", "proj"), ("k", "proj"), ("plain", "tr"),
                 ("sig", "proj"), ("sig", "pool_gate"), ("sig", "proj"))
    ctx_kinds = (("skip", ""), ("q", "proj"), ("k", "proj"), ("plain", "tr"))
    l_lat = min(SCAN_CHUNK, n)
    l_ctx = min(SCAN_CHUNK, ctx.shape[1])
    proj, vt, gates, ypool = _in_proj(x, mod, (0, 1), norm_mix, w_main, w_gate, conv_w[0], conv_b, b_gate,
                                      lat_kinds, q_scale, l_lat, pool=(_band_matrices(), w_pool_b, pool_scale))
    proj_c, vt_c, gates_c = _in_proj(ctx, mod_ctx, (0, 1), norm_mix, w_main, w_gate, conv_w[0],
                                     conv_b, b_gate, ctx_kinds, q_scale, l_ctx)

    gcol_c, grow_c, m_c = _gateprep(gates_c, l_ctx, None)
    gcol, grow, _ = _gateprep(gates, l_lat, m_c)
    (state,) = _mlstm(proj_c, 0, 1, vt_c, gcol_c, grow_c, l_ctx, None, False, True)
    h_dirs = _mlstm(proj, 0, 1, vt, gcol, grow, l_lat, state, True, False)

    return _tail(x, h_dirs[0], h_dirs[1], ypool, proj, 2, 3, mod, mh_gain, _cast_bf16(w_out[0]), norm_ffn,
                 _cast_bf16(w_ffn_in[0]), _cast_bf16(w_ffn_out[0]), norm_final[None])
```
